```python
import jax, jax.numpy as jnp
from jax import lax
import numpy as np

D_MODEL = 1024
BATCH = 8
SEQ = 8192
DEPTH = 4

D_MIX = D_MODEL
N_MIXERS = 4
D_GROUP = D_MIX // N_MIXERS
HEAD_DIM = 64
N_HEADS = D_GROUP // HEAD_DIM
CONF_KERNEL = 31
SHORT_KERNEL = 3
POOL_WINDOWS = (2, 4, 8, 16)
POOL_GROUP = D_GROUP // len(POOL_WINDOWS)
CHUNK = 128
D_FF = 2816
N_IN_PIECES = 8
D_IN = N_IN_PIECES * D_GROUP
FFN_RESIDUAL = 0.5
EPS = 1e-6

kernel_name = "hybrid_macaron_parallel_conv_pool_gmlp"


def rmsnorm(x, g):
    xf = x.astype(jnp.float32)
    y = xf * lax.rsqrt(jnp.mean(xf * xf, axis=-1, keepdims=True) + EPS)
    return (y * g.astype(jnp.float32)).astype(x.dtype)


def layernorm(x, g, b):
    xf = x.astype(jnp.float32)
    mu = jnp.mean(xf, axis=-1, keepdims=True)
    xc = xf - mu
    y = xc * lax.rsqrt(jnp.mean(xc * xc, axis=-1, keepdims=True) + EPS)
    return (y * g.astype(jnp.float32) + b.astype(jnp.float32)).astype(x.dtype)


def causal_depthwise_conv(x, w):
    k, c = w.shape
    return lax.conv_general_dilated(
        x, w[:, None, :].astype(x.dtype), window_strides=(1,), padding=[(k - 1, 0)],
        dimension_numbers=("NWC", "WIO", "NWC"), feature_group_count=c)


def swiglu(h, w1, w3, w2):
    return (jax.nn.silu(h @ w1) * (h @ w3)) @ w2


def conformer_conv(val, gate, conv_w, conv_b, ln_g, ln_b):
    y = val * jax.nn.sigmoid(gate)
    y = causal_depthwise_conv(y, conv_w) + conv_b
    return jax.nn.silu(layernorm(y, ln_g, ln_b))


def short_gated_conv(b_gate, c_gate, xv, conv_w):
    return b_gate * causal_depthwise_conv(c_gate * xv, conv_w)


def multiscale_pool(xp, pool_w, pool_scale):
    bsz, s, _ = xp.shape
    xf = xp.astype(jnp.float32)
    cs = jnp.cumsum(xf, axis=1)
    pos = jnp.arange(1, s + 1, dtype=jnp.float32)[:, None]
    outs = []
    for g, w in enumerate(POOL_WINDOWS):
        sl = slice(g * POOL_GROUP, (g + 1) * POOL_GROUP)
        c = cs[..., sl]
        lagged = jnp.pad(c, ((0, 0), (w, 0), (0, 0)))[:, :s]
        mean = (c - lagged) / jnp.minimum(pos, float(w))
        outs.append(mean - xf[..., sl])
    d = jnp.stack(outs, axis=2).astype(xp.dtype)
    y = jnp.einsum("bsgc,gcd->bsgd", d, pool_w).reshape(bsz, s, D_GROUP)
    return y * pool_scale


def chunked_spatial_gating(u, v, ln_g, ln_b, w_s, b_s):
    bsz, s, _ = v.shape
    v = layernorm(v, ln_g, ln_b)
    vc = v.reshape(bsz, s // CHUNK, CHUNK, N_HEADS, HEAD_DIM)
    mask = jnp.tril(jnp.ones((CHUNK, CHUNK), dtype=bool))
    ws = jnp.where(mask[None], w_s, 0.0).astype(v.dtype)
    mixed = jnp.einsum("hts,bnshc->bnthc", ws, vc) + b_s.T[None, None, :, :, None]
    return u * mixed.reshape(bsz, s, D_GROUP)


def setup_inputs(seed: int = 0) -> dict:
    key = jax.random.key(seed)
    ks = iter(jax.random.split(key, 32))

    def nrm(shape, scale):
        return jax.random.normal(next(ks), shape, dtype=jnp.float32) * scale

    def gain(shape):
        return 1.0 + nrm(shape, 0.02)

    L = DEPTH
    return {
        "x": nrm((BATCH, SEQ, D_MODEL), 1.0),
        "ffn1_norm": gain((L, D_MODEL)),
        "ffn1_w1": nrm((L, D_MODEL, D_FF), D_MODEL ** -0.5),
        "ffn1_w3": nrm((L, D_MODEL, D_FF), D_MODEL ** -0.5),
        "ffn1_w2": nrm((L, D_FF, D_MODEL), D_FF ** -0.5),
        "mix_norm": gain((L, D_MODEL)),
        "w_in": nrm((L, D_MODEL, D_IN), D_MODEL ** -0.5),
        "conf_conv_w": nrm((L, CONF_KERNEL, D_GROUP), CONF_KERNEL ** -0.5),
        "conf_conv_b": nrm((L, D_GROUP), 0.02),
        "conf_ln_g": gain((L, D_GROUP)),
        "conf_ln_b": nrm((L, D_GROUP), 0.02),
        "sconv_w": nrm((L, SHORT_KERNEL, D_GROUP), SHORT_KERNEL ** -0.5),
        "pool_w": nrm((L, len(POOL_WINDOWS), POOL_GROUP, POOL_GROUP), POOL_GROUP ** -0.5),
        "pool_scale": 1.0 + nrm((L, D_GROUP), 0.1),
        "gmlp_ln_g": gain((L, D_GROUP)),
        "gmlp_ln_b": nrm((L, D_GROUP), 0.02),
        "gmlp_w_s": nrm((L, N_HEADS, CHUNK, CHUNK), CHUNK ** -0.5),
        "gmlp_b_s": 1.0 + nrm((L, N_HEADS, CHUNK), 0.02),
        "w_out": nrm((L, D_MIX, D_MODEL), D_MIX ** -0.5),
        "ffn2_norm": gain((L, D_MODEL)),
        "ffn2_w1": nrm((L, D_MODEL, D_FF), D_MODEL ** -0.5),
        "ffn2_w3": nrm((L, D_MODEL, D_FF), D_MODEL ** -0.5),
        "ffn2_w2": nrm((L, D_FF, D_MODEL), D_FF ** -0.5),
        "final_norm": gain((D_MODEL,)),
    }


def reference(x, ffn1_norm, ffn1_w1, ffn1_w3, ffn1_w2, mix_norm, w_in,
              conf_conv_w, conf_conv_b, conf_ln_g, conf_ln_b, sconv_w,
              pool_w, pool_scale, gmlp_ln_g, gmlp_ln_b, gmlp_w_s, gmlp_b_s,
              w_out, ffn2_norm, ffn2_w1, ffn2_w3, ffn2_w2, final_norm):
    for l in range(DEPTH):
        h = rmsnorm(x, ffn1_norm[l])
        x = x + FFN_RESIDUAL * swiglu(h, ffn1_w1[l], ffn1_w3[l], ffn1_w2[l])

        h = rmsnorm(x, mix_norm[l])
        p = h @ w_in[l]
        a_val, a_gate, s_b, s_c, s_x, pool_in, g_u, g_v = jnp.split(p, N_IN_PIECES, axis=-1)

        y_a = conformer_conv(a_val, a_gate, conf_conv_w[l], conf_conv_b[l],
                             conf_ln_g[l], conf_ln_b[l])
        y_b = short_gated_conv(s_b, s_c, s_x, sconv_w[l])
        y_c = multiscale_pool(pool_in, pool_w[l], pool_scale[l])
        y_d = chunked_spatial_gating(g_u, g_v, gmlp_ln_g[l], gmlp_ln_b[l],
                                     gmlp_w_s[l], gmlp_b_s[l])

        mix = jnp.concatenate([y_a, y_b, y_c, y_d], axis=-1)
        x = x + mix @ w_out[l]

        h = rmsnorm(x, ffn2_norm[l])
        x = x + FFN_RESIDUAL * swiglu(h, ffn2_w1[l], ffn2_w3[l], ffn2_w2[l])

    return rmsnorm(x, final_norm)
```

```python
import functools

import jax
import jax.numpy as jnp
from jax import lax
from jax.experimental import pallas as pl
from jax.experimental.pallas import tpu as pltpu

EPS = 1e-6
FFN_RESIDUAL = 0.5
D_GROUP = 256
N_HEADS = 4
HEAD_DIM = 64
CONF_KERNEL = 31
SHORT_KERNEL = 3
POOL_WINDOWS = (2, 4, 8, 16)
POOL_GROUP = 64
CHUNK = 128

SUBLANES = 8
CONF_HALO = 32
SHORT_HALO = 8
POOL_HALO = 16

FFN_TILE = 512
FFN_CHUNK = 256
MIX_TILE = 512
VMEM_LIMIT_BYTES = 56 * 1024 * 1024

_bf16 = jnp.bfloat16
_f32 = jnp.float32


def _rms_scale(x):
    return lax.rsqrt(jnp.mean(x * x, axis=-1, keepdims=True) + EPS)


def _layernorm(x, g, b):
    mu = jnp.mean(x, axis=-1, keepdims=True)
    xc = x - mu
    var = jnp.mean(xc * xc, axis=-1, keepdims=True)
    return xc * lax.rsqrt(var + EPS) * g + b


def _silu(x):
    return x * jax.nn.sigmoid(x)


def _ffn_kernel(x_ref, g_ref, w1_ref, w3_ref, w2_ref, fg_ref, o_ref, h_ref, gate_ref,
                *, d_ff, apply_final_norm):
    x = x_ref[...]
    h_ref[...] = (x * _rms_scale(x) * g_ref[...]).astype(_bf16)
    for c0 in range(0, d_ff, FFN_CHUNK):
        hb = h_ref[...]
        a = jnp.dot(hb, w1_ref[:, c0:c0 + FFN_CHUNK], preferred_element_type=_f32)
        b = jnp.dot(hb, w3_ref[:, c0:c0 + FFN_CHUNK], preferred_element_type=_f32)
        gate_ref[:, c0:c0 + FFN_CHUNK] = (_silu(a) * b).astype(_bf16)
    y = jnp.dot(gate_ref[...], w2_ref[...], preferred_element_type=_f32)
    out = x_ref[...] + FFN_RESIDUAL * y
    if apply_final_norm:
        out = out * _rms_scale(out) * fg_ref[...]
    o_ref[...] = out


def _ffn_call(x2d, g, w1, w3, w2, final_g, *, apply_final_norm):
    t, d = x2d.shape
    d_ff = w1.shape[1]
    tm = FFN_TILE
    assert t % tm == 0 and d_ff % FFN_CHUNK == 0
    const = lambda i: (0, 0)
    resident = functools.partial(pl.BlockSpec, index_map=const, pipeline_mode=pl.Buffered(1))
    kern = functools.partial(_ffn_kernel, d_ff=d_ff, apply_final_norm=apply_final_norm)
    return pl.pallas_call(
        kern,
        out_shape=jax.ShapeDtypeStruct((t, d), _f32),
        grid=(t // tm,),
        in_specs=[
            pl.BlockSpec((tm, d), lambda i: (i, 0)),
            resident((1, d)),
            resident((d, d_ff)),
            resident((d, d_ff)),
            resident((d_ff, d)),
            resident((1, d)),
        ],
        out_specs=pl.BlockSpec((tm, d), lambda i: (i, 0)),
        scratch_shapes=[
            pltpu.VMEM((tm, d), _bf16),
            pltpu.VMEM((tm, d_ff), _bf16),
        ],
        compiler_params=pltpu.CompilerParams(
            dimension_semantics=("arbitrary",),
            vmem_limit_bytes=VMEM_LIMIT_BYTES),
        name="swiglu_half_step",
    )(x2d, g, w1, w3, w2, final_g)


def _mixer_kernel(x_ref, g_ref, win_ref, cw_ref, cb_ref, clg_ref, clb_ref, sw_ref,
                  pw_ref, ps_ref, glg_ref, glb_ref, ws_ref, bs_ref, wout_ref,
                  o_ref, h_ref, mix_ref, ycar_ref, zcar_ref, pcar_ref):
    tm = x_ref.shape[1]
    j = pl.program_id(1)

    @pl.when(j == 0)
    def _():
        ycar_ref[...] = jnp.zeros_like(ycar_ref)
        zcar_ref[...] = jnp.zeros_like(zcar_ref)
        pcar_ref[...] = jnp.zeros_like(pcar_ref)

    x = x_ref[0]
    h_ref[...] = (x * _rms_scale(x) * g_ref[...]).astype(_bf16)

    def proj(piece0, n_pieces):
        lo = piece0 * D_GROUP
        return jnp.dot(h_ref[...], win_ref[:, lo:lo + n_pieces * D_GROUP],
                       preferred_element_type=_f32)

    p = proj(0, 2)
    y = p[:, :D_GROUP] * jax.nn.sigmoid(p[:, D_GROUP:])
    yext = jnp.concatenate([ycar_ref[...], y], axis=0)
    ycar_ref[...] = y[tm - CONF_HALO:, :]
    acc = jnp.zeros((tm, D_GROUP), _f32)
    for b in range(SUBLANES):
        yb = yext if b == 0 else pltpu.roll(yext, b, 0)
        for a in range(CONF_HALO // SUBLANES):
            shift = SUBLANES * a + b
            if shift >= CONF_KERNEL:
                continue
            k = CONF_KERNEL - 1 - shift
            lo = CONF_HALO - SUBLANES * a
            acc = acc + cw_ref[k:k + 1, :] * yb[lo:lo + tm, :]
    conv = acc + cb_ref[...]
    mix_ref[:, 0:D_GROUP] = _silu(_layernorm(conv, clg_ref[...], clb_ref[...])).astype(_bf16)

    p = proj(2, 3)
    z = p[:, D_GROUP:2 * D_GROUP] * p[:, 2 * D_GROUP:]
    zext = jnp.concatenate([zcar_ref[...], z], axis=0)
    zcar_ref[...] = z[tm - SHORT_HALO:, :]
    conv = sw_ref[SHORT_KERNEL - 1:SHORT_KERNEL, :] * z
    for shift in range(1, SHORT_KERNEL):
        k = SHORT_KERNEL - 1 - shift
        conv = conv + sw_ref[k:k + 1, :] * pltpu.roll(zext, shift, 0)[SHORT_HALO:, :]
    mix_ref[:, D_GROUP:2 * D_GROUP] = (p[:, :D_GROUP] * conv).astype(_bf16)

    xp = proj(5, 1)
    xe = jnp.concatenate([pcar_ref[...], xp], axis=0)
    pcar_ref[...] = xp[tm - POOL_HALO:, :]
    lane = lax.broadcasted_iota(jnp.int32, (tm, D_GROUP), 1)
    row = lax.broadcasted_iota(jnp.int32, (tm, D_GROUP), 0)
    pos = (j * tm + row + 1).astype(_f32)
    run, width = xe, 1
    wsum = None
    wlane = None
    for gi, w in enumerate(POOL_WINDOWS):
        while width < w:
            run = run + pltpu.roll(run, width, 0)
            width *= 2
        cur = run[POOL_HALO:, :]
        if wsum is None:
            wsum, wlane = cur, jnp.full((tm, D_GROUP), float(w), _f32)
        else:
            in_group = lane >= gi * POOL_GROUP
            wsum = jnp.where(in_group, cur, wsum)
            wlane = jnp.where(in_group, float(w), wlane)
    dpool = wsum / jnp.minimum(pos, wlane) - xp
    yc = jnp.dot(dpool.astype(_bf16), pw_ref[...], preferred_element_type=_f32)
    mix_ref[:, 2 * D_GROUP:3 * D_GROUP] = (yc * ps_ref[...]).astype(_bf16)

    p = proj(6, 2)
    v = _layernorm(p[:, D_GROUP:], glg_ref[...], glb_ref[...]).astype(_bf16)
    wr = lax.broadcasted_iota(jnp.int32, (N_HEADS * CHUNK, CHUNK), 0) % CHUNK
    wc = lax.broadcasted_iota(jnp.int32, (N_HEADS * CHUNK, CHUNK), 1)
    ws = jnp.where(wc <= wr, ws_ref[...], 0.0).astype(_bf16)
    hlane = lax.broadcasted_iota(jnp.int32, (CHUNK, D_GROUP), 1) // HEAD_DIM
    for n in range(tm // CHUNK):
        rows = slice(n * CHUNK, (n + 1) * CHUNK)
        r = jnp.dot(ws, v[rows, :], preferred_element_type=_f32)
        mixed = r[0:CHUNK, :]
        for hd in range(1, N_HEADS):
            mixed = jnp.where(hlane == hd, r[hd * CHUNK:(hd + 1) * CHUNK, :], mixed)
        mixed = mixed + bs_ref[...]
        mix_ref[rows, 3 * D_GROUP:] = (p[rows, :D_GROUP] * mixed).astype(_bf16)

    o_ref[0] = x_ref[0] + jnp.dot(mix_ref[...], wout_ref[...], preferred_element_type=_f32)


def _mixer_call(x, g, w_in, cw, cb, clg, clb, sw, pw, ps, glg, glb, ws, bs, w_out):
    bsz, s, d = x.shape
    tm = MIX_TILE
    assert s % tm == 0 and tm % CHUNK == 0 and tm >= CONF_HALO
    small = [g, w_in, cw, cb, clg, clb, sw, pw, ps, glg, glb, ws, bs, w_out]
    const = lambda b, j: (0, 0)
    resident = functools.partial(pl.BlockSpec, index_map=const, pipeline_mode=pl.Buffered(1))
    return pl.pallas_call(
        _mixer_kernel,
        out_shape=jax.ShapeDtypeStruct((bsz, s, d), _f32),
        grid=(bsz, s // tm),
        in_specs=[pl.BlockSpec((1, tm, d), lambda b, j: (b, j, 0))]
                 + [resident(a.shape) for a in small],
        out_specs=pl.BlockSpec((1, tm, d), lambda b, j: (b, j, 0)),
        scratch_shapes=[
            pltpu.VMEM((tm, d), _bf16),
            pltpu.VMEM((tm, 4 * D_GROUP), _bf16),
            pltpu.VMEM((CONF_HALO, D_GROUP), _f32),
            pltpu.VMEM((SHORT_HALO, D_GROUP), _f32),
            pltpu.VMEM((POOL_HALO, D_GROUP), _f32),
        ],
        compiler_params=pltpu.CompilerParams(
            dimension_semantics=("arbitrary", "arbitrary"),
            vmem_limit_bytes=VMEM_LIMIT_BYTES),
        name="mixer_block",
    )(x, *small)


def _block_diag(pool_w):
    g, c, d = pool_w.shape
    eye = jnp.eye(g, dtype=pool_w.dtype)
    return (eye[:, None, :, None] * pool_w[:, :, None, :]).reshape(g * c, g * d)


def kernel(x, ffn1_norm, ffn1_w1, ffn1_w3, ffn1_w2, mix_norm, w_in, conf_conv_w, conf_conv_b, conf_ln_g, conf_ln_b, sconv_w, pool_w, pool_scale, gmlp_ln_g, gmlp_ln_b, gmlp_w_s, gmlp_b_s, w_out, ffn2_norm, ffn2_w1, ffn2_w3, ffn2_w2, final_norm):
    bsz, s, d = x.shape
    depth = ffn1_norm.shape[0]
    row = lambda a: a.reshape(1, -1)
    fg = row(final_norm)
    for l in range(depth):
        x = _ffn_call(x.reshape(bsz * s, d), row(ffn1_norm[l]), ffn1_w1[l].astype(_bf16),
                      ffn1_w3[l].astype(_bf16), ffn1_w2[l].astype(_bf16), fg,
                      apply_final_norm=False).reshape(bsz, s, d)
        x = _mixer_call(
            x, row(mix_norm[l]), w_in[l].astype(_bf16), conf_conv_w[l], row(conf_conv_b[l]),
            row(conf_ln_g[l]), row(conf_ln_b[l]), sconv_w[l],
            _block_diag(pool_w[l]).astype(_bf16), row(pool_scale[l]),
            row(gmlp_ln_g[l]), row(gmlp_ln_b[l]),
            gmlp_w_s[l].reshape(N_HEADS * CHUNK, CHUNK),
            jnp.repeat(gmlp_b_s[l].T, HEAD_DIM, axis=1),
            w_out[l].astype(_bf16))
        x = _ffn_call(x.reshape(bsz * s, d), row(ffn2_norm[l]), ffn2_w1[l].astype(_bf16),
                      ffn2_w3[l].astype(_bf16), ffn2_w2[l].astype(_bf16), fg,
                      apply_final_norm=(l == depth - 1)).reshape(bsz, s, d)
    return x
```

```python
import functools

import jax
import jax.numpy as jnp
from jax import lax
from jax.experimental import pallas as pl
from jax.experimental.pallas import tpu as pltpu

EPS = 1e-6
FFN_RESIDUAL = 0.5
D_GROUP = 256
N_HEADS = 4
HEAD_DIM = 64
CONF_KERNEL = 31
SHORT_KERNEL = 3
POOL_WINDOWS = (2, 4, 8, 16)
POOL_GROUP = 64
CHUNK = 128

SUBLANES = 8
CONF_HALO = 32
SHORT_HALO = 8
POOL_HALO = 16

FFN_TILE = 1024
FFN_SUB = 512
FFN_CHUNK = 256
MIX_TILE = 512
VMEM_LIMIT_BYTES = 56 * 1024 * 1024

_bf16 = jnp.bfloat16
_f32 = jnp.float32


def _rms_scale(x):
    return lax.rsqrt(jnp.mean(x * x, axis=-1, keepdims=True) + EPS)


def _layernorm(x, g, b):
    mu = jnp.mean(x, axis=-1, keepdims=True)
    xc = x - mu
    var = jnp.mean(xc * xc, axis=-1, keepdims=True)
    return xc * lax.rsqrt(var + EPS) * g + b


def _silu(x):
    return x * jax.nn.sigmoid(x)


def _layer_spec(arr, layer):
    zeros = (0,) * (arr.ndim - 1)
    return pl.BlockSpec((None,) + arr.shape[1:], lambda *_: (layer,) + zeros,
                        pipeline_mode=pl.Buffered(1))


def _ffn_kernel(x_ref, g_ref, w1_ref, w3_ref, w2_ref, fg_ref, o_ref, h_ref, gate_ref,
                *, d_ff, apply_final_norm):
    tm = x_ref.shape[0]
    for r0 in range(0, tm, FFN_SUB):
        rows = slice(r0, r0 + FFN_SUB)
        x = x_ref[rows, :]
        h_ref[rows, :] = (x * _rms_scale(x) * g_ref[...]).astype(_bf16)
    for r0 in range(0, tm, FFN_SUB):
        rows = slice(r0, r0 + FFN_SUB)
        for c0 in range(0, d_ff, FFN_CHUNK):
            hb = h_ref[rows, :]
            a = jnp.dot(hb, w1_ref[:, c0:c0 + FFN_CHUNK], preferred_element_type=_f32)
            b = jnp.dot(hb, w3_ref[:, c0:c0 + FFN_CHUNK], preferred_element_type=_f32)
            gate_ref[rows, c0:c0 + FFN_CHUNK] = (_silu(a) * b).astype(_bf16)
        y = jnp.dot(gate_ref[rows, :], w2_ref[...], preferred_element_type=_f32)
        out = x_ref[rows, :] + FFN_RESIDUAL * y
        if apply_final_norm:
            out = out * _rms_scale(out) * fg_ref[...]
        o_ref[rows, :] = out


def _ffn_call(x2d, layer, g, w1, w3, w2, final_g, *, apply_final_norm):
    t, d = x2d.shape
    d_ff = w1.shape[2]
    tm = FFN_TILE
    assert t % tm == 0 and tm % FFN_SUB == 0 and d_ff % FFN_CHUNK == 0
    kern = functools.partial(_ffn_kernel, d_ff=d_ff, apply_final_norm=apply_final_norm)
    params = [g, w1, w3, w2]
    return pl.pallas_call(
        kern,
        out_shape=jax.ShapeDtypeStruct((t, d), _f32),
        grid=(t // tm,),
        in_specs=[pl.BlockSpec((tm, d), lambda i: (i, 0))]
                 + [_layer_spec(a, layer) for a in params]
                 + [pl.BlockSpec((1, d), lambda i: (0, 0), pipeline_mode=pl.Buffered(1))],
        out_specs=pl.BlockSpec((tm, d), lambda i: (i, 0)),
        scratch_shapes=[
            pltpu.VMEM((tm, d), _bf16),
            pltpu.VMEM((tm, d_ff), _bf16),
        ],
        compiler_params=pltpu.CompilerParams(
            dimension_semantics=("arbitrary",),
            vmem_limit_bytes=VMEM_LIMIT_BYTES),
        name="swiglu_half_step",
    )(x2d, *params, final_g)


def _mixer_kernel(x_ref, g_ref, win_ref, cw_ref, cb_ref, clg_ref, clb_ref, sw_ref,
                  pw_ref, ps_ref, glg_ref, glb_ref, ws_ref, bs_ref, wout_ref,
                  o_ref, h_ref, mix_ref, ycar_ref, zcar_ref, pcar_ref):
    tm = x_ref.shape[1]
    j = pl.program_id(1)

    @pl.when(j == 0)
    def _():
        ycar_ref[...] = jnp.zeros_like(ycar_ref)
        zcar_ref[...] = jnp.zeros_like(zcar_ref)
        pcar_ref[...] = jnp.zeros_like(pcar_ref)

    x = x_ref[0]
    h_ref[...] = (x * _rms_scale(x) * g_ref[...]).astype(_bf16)

    def proj(piece0, n_pieces):
        lo = piece0 * D_GROUP
        return jnp.dot(h_ref[...], win_ref[:, lo:lo + n_pieces * D_GROUP],
                       preferred_element_type=_f32)

    p = proj(0, 2)
    y = p[:, :D_GROUP] * jax.nn.sigmoid(p[:, D_GROUP:])
    yext = jnp.concatenate([ycar_ref[...], y], axis=0)
    ycar_ref[...] = y[tm - CONF_HALO:, :]
    acc = jnp.zeros((tm, D_GROUP), _f32)
    for b in range(SUBLANES):
        yb = yext if b == 0 else pltpu.roll(yext, b, 0)
        for a in range(CONF_HALO // SUBLANES):
            shift = SUBLANES * a + b
            if shift >= CONF_KERNEL:
                continue
            k = CONF_KERNEL - 1 - shift
            lo = CONF_HALO - SUBLANES * a
            acc = acc + cw_ref[k:k + 1, :] * yb[lo:lo + tm, :]
    conv = acc + cb_ref[...]
    mix_ref[:, 0:D_GROUP] = _silu(_layernorm(conv, clg_ref[...], clb_ref[...])).astype(_bf16)

    p = proj(2, 3)
    z = p[:, D_GROUP:2 * D_GROUP] * p[:, 2 * D_GROUP:]
    zext = jnp.concatenate([zcar_ref[...], z], axis=0)
    zcar_ref[...] = z[tm - SHORT_HALO:, :]
    conv = sw_ref[SHORT_KERNEL - 1:SHORT_KERNEL, :] * z
    for shift in range(1, SHORT_KERNEL):
        k = SHORT_KERNEL - 1 - shift
        conv = conv + sw_ref[k:k + 1, :] * pltpu.roll(zext, shift, 0)[SHORT_HALO:, :]
    mix_ref[:, D_GROUP:2 * D_GROUP] = (p[:, :D_GROUP] * conv).astype(_bf16)

    xp = proj(5, 1)
    xe = jnp.concatenate([pcar_ref[...], xp], axis=0)
    pcar_ref[...] = xp[tm - POOL_HALO:, :]
    lane = lax.broadcasted_iota(jnp.int32, (tm, D_GROUP), 1)
    row = lax.broadcasted_iota(jnp.int32, (tm, D_GROUP), 0)
    pos = (j * tm + row + 1).astype(_f32)
    run, width = xe, 1
    wsum = None
    wlane = None
    for gi, w in enumerate(POOL_WINDOWS):
        while width < w:
            run = run + pltpu.roll(run, width, 0)
            width *= 2
        cur = run[POOL_HALO:, :]
        if wsum is None:
            wsum, wlane = cur, jnp.full((tm, D_GROUP), float(w), _f32)
        else:
            in_group = lane >= gi * POOL_GROUP
            wsum = jnp.where(in_group, cur, wsum)
            wlane = jnp.where(in_group, float(w), wlane)
    dpool = wsum / jnp.minimum(pos, wlane) - xp
    yc = jnp.dot(dpool.astype(_bf16), pw_ref[...], preferred_element_type=_f32)
    mix_ref[:, 2 * D_GROUP:3 * D_GROUP] = (yc * ps_ref[...]).astype(_bf16)

    p = proj(6, 2)
    v = _layernorm(p[:, D_GROUP:], glg_ref[...], glb_ref[...]).astype(_bf16)
    wr = lax.broadcasted_iota(jnp.int32, (N_HEADS * CHUNK, CHUNK), 0) % CHUNK
    wc = lax.broadcasted_iota(jnp.int32, (N_HEADS * CHUNK, CHUNK), 1)
    ws = jnp.where(wc <= wr, ws_ref[...], 0.0).astype(_bf16)
    hlane = lax.broadcasted_iota(jnp.int32, (CHUNK, D_GROUP), 1) // HEAD_DIM
    for n in range(tm // CHUNK):
        rows = slice(n * CHUNK, (n + 1) * CHUNK)
        r = jnp.dot(ws, v[rows, :], preferred_element_type=_f32)
        mixed = r[0:CHUNK, :]
        for hd in range(1, N_HEADS):
            mixed = jnp.where(hlane == hd, r[hd * CHUNK:(hd + 1) * CHUNK, :], mixed)
        mixed = mixed + bs_ref[...]
        mix_ref[rows, 3 * D_GROUP:] = (p[rows, :D_GROUP] * mixed).astype(_bf16)

    o_ref[0] = x_ref[0] + jnp.dot(mix_ref[...], wout_ref[...], preferred_element_type=_f32)


def _mixer_call(x, layer, params):
    bsz, s, d = x.shape
    tm = MIX_TILE
    assert s % tm == 0 and tm % CHUNK == 0 and tm >= CONF_HALO
    return pl.pallas_call(
        _mixer_kernel,
        out_shape=jax.ShapeDtypeStruct((bsz, s, d), _f32),
        grid=(bsz, s // tm),
        in_specs=[pl.BlockSpec((1, tm, d), lambda b, j: (b, j, 0))]
                 + [_layer_spec(a, layer) for a in params],
        out_specs=pl.BlockSpec((1, tm, d), lambda b, j: (b, j, 0)),
        scratch_shapes=[
            pltpu.VMEM((tm, d), _bf16),
            pltpu.VMEM((tm, 4 * D_GROUP), _bf16),
            pltpu.VMEM((CONF_HALO, D_GROUP), _f32),
            pltpu.VMEM((SHORT_HALO, D_GROUP), _f32),
            pltpu.VMEM((POOL_HALO, D_GROUP), _f32),
        ],
        compiler_params=pltpu.CompilerParams(
            dimension_semantics=("arbitrary", "arbitrary"),
            vmem_limit_bytes=VMEM_LIMIT_BYTES),
        name="mixer_block",
    )(x, *params)


def _block_diag(pool_w):
    nl, g, c, d = pool_w.shape
    eye = jnp.eye(g, dtype=pool_w.dtype)
    return (eye[None, :, None, :, None] * pool_w[:, :, :, None, :]).reshape(nl, g * c, g * d)


def kernel(x, ffn1_norm, ffn1_w1, ffn1_w3, ffn1_w2, mix_norm, w_in, conf_conv_w, conf_conv_b, conf_ln_g, conf_ln_b, sconv_w, pool_w, pool_scale, gmlp_ln_g, gmlp_ln_b, gmlp_w_s, gmlp_b_s, w_out, ffn2_norm, ffn2_w1, ffn2_w3, ffn2_w2, final_norm):
    bsz, s, d = x.shape
    depth = ffn1_norm.shape[0]
    rows = lambda a: a.reshape(depth, 1, -1)
    cast = lambda a: a.astype(_bf16)
    fg = final_norm.reshape(1, -1)
    ffn1 = (rows(ffn1_norm), cast(ffn1_w1), cast(ffn1_w3), cast(ffn1_w2))
    ffn2 = (rows(ffn2_norm), cast(ffn2_w1), cast(ffn2_w3), cast(ffn2_w2))
    mixer = [
        rows(mix_norm), cast(w_in), conf_conv_w, rows(conf_conv_b), rows(conf_ln_g),
        rows(conf_ln_b), sconv_w, cast(_block_diag(pool_w)), rows(pool_scale),
        rows(gmlp_ln_g), rows(gmlp_ln_b),
        gmlp_w_s.reshape(depth, N_HEADS * CHUNK, CHUNK),
        jnp.repeat(jnp.swapaxes(gmlp_b_s, 1, 2), HEAD_DIM, axis=2),
        cast(w_out),
    ]
    x2d = x.reshape(bsz * s, d)
    for l in range(depth):
        x2d = _ffn_call(x2d, l, *ffn1, fg, apply_final_norm=False)
        x2d = _mixer_call(x2d.reshape(bsz, s, d), l, mixer).reshape(bsz * s, d)
        x2d = _ffn_call(x2d, l, *ffn2, fg, apply_final_norm=(l == depth - 1))
    return x2d.reshape(bsz, s, d)
```

```python
import functools

import jax
import jax.numpy as jnp
from jax import lax
from jax.experimental import pallas as pl
from jax.experimental.pallas import tpu as pltpu

EPS = 1e-6
FFN_RESIDUAL = 0.5
D_GROUP = 256
N_HEADS = 4
HEAD_DIM = 64
CONF_KERNEL = 31
SHORT_KERNEL = 3
POOL_WINDOWS = (2, 4, 8, 16)
POOL_GROUP = 64
CHUNK = 128

SUBLANES = 8
CONF_HALO = 32
SHORT_HALO = 8
POOL_HALO = 16

FFN_TILE = 1024
FFN_SUB = 512
FFN_CHUNK = 256
MIX_TILE = 1024
MIX_SUB = 512
MIX_PROJ_ORDER = ("conf", "short", "pool", "gate")
MIX_BLOCK_ORDER = ("next:conf", "next:short", "next:pool", "mix:conf", "mix:short", "mix:pool",
                   "mix:gate@0", "mix:gate@1", "mix:gate@2", "mix:gate@3", "out:", "next:gate")
VMEM_LIMIT_BYTES = 56 * 1024 * 1024

_bf16 = jnp.bfloat16
_f32 = jnp.float32


def _rms_scale(x):
    return lax.rsqrt(jnp.mean(x * x, axis=-1, keepdims=True) + EPS)


def _layernorm(x, g, b):
    mu = jnp.mean(x, axis=-1, keepdims=True)
    xc = x - mu
    var = jnp.mean(xc * xc, axis=-1, keepdims=True)
    return xc * lax.rsqrt(var + EPS) * g + b


def _silu(x):
    return x * jax.nn.sigmoid(x)


def _layer_spec(arr, layer):
    zeros = (0,) * (arr.ndim - 1)
    return pl.BlockSpec((None,) + arr.shape[1:], lambda *_: (layer,) + zeros,
                        pipeline_mode=pl.Buffered(1))


def _ffn_kernel(x_ref, g_ref, w1_ref, w3_ref, w2_ref, fg_ref, o_ref, h_ref, gate_ref,
                *, d_ff, apply_final_norm):
    tm = x_ref.shape[0]
    for r0 in range(0, tm, FFN_SUB):
        rows = slice(r0, r0 + FFN_SUB)
        x = x_ref[rows, :]
        h_ref[rows, :] = (x * _rms_scale(x) * g_ref[...]).astype(_bf16)
    for r0 in range(0, tm, FFN_SUB):
        rows = slice(r0, r0 + FFN_SUB)
        for c0 in range(0, d_ff, FFN_CHUNK):
            hb = h_ref[rows, :]
            a = jnp.dot(hb, w1_ref[:, c0:c0 + FFN_CHUNK], preferred_element_type=_f32)
            b = jnp.dot(hb, w3_ref[:, c0:c0 + FFN_CHUNK], preferred_element_type=_f32)
            gate_ref[rows, c0:c0 + FFN_CHUNK] = (_silu(a) * b).astype(_bf16)
        y = jnp.dot(gate_ref[rows, :], w2_ref[...], preferred_element_type=_f32)
        out = x_ref[rows, :] + FFN_RESIDUAL * y
        if apply_final_norm:
            out = out * _rms_scale(out) * fg_ref[...]
        o_ref[rows, :] = out


def _ffn_call(x2d, layer, g, w1, w3, w2, final_g, *, apply_final_norm):
    t, d = x2d.shape
    d_ff = w1.shape[2]
    tm = FFN_TILE
    assert t % tm == 0 and tm % FFN_SUB == 0 and d_ff % FFN_CHUNK == 0
    kern = functools.partial(_ffn_kernel, d_ff=d_ff, apply_final_norm=apply_final_norm)
    params = [g, w1, w3, w2]
    return pl.pallas_call(
        kern,
        out_shape=jax.ShapeDtypeStruct((t, d), _f32),
        grid=(t // tm,),
        in_specs=[pl.BlockSpec((tm, d), lambda i: (i, 0))]
                 + [_layer_spec(a, layer) for a in params]
                 + [pl.BlockSpec((1, d), lambda i: (0, 0), pipeline_mode=pl.Buffered(1))],
        out_specs=pl.BlockSpec((tm, d), lambda i: (i, 0)),
        scratch_shapes=[
            pltpu.VMEM((tm, d), _bf16),
            pltpu.VMEM((tm, d_ff), _bf16),
        ],
        compiler_params=pltpu.CompilerParams(
            dimension_semantics=("arbitrary",),
            vmem_limit_bytes=VMEM_LIMIT_BYTES),
        name="swiglu_half_step",
    )(x2d, *params, final_g)


def _mixer_kernel(x_ref, g_ref, win_ref, cw_ref, cb_ref, clg_ref, clb_ref, sw_ref,
                  pw_ref, ps_ref, glg_ref, glb_ref, ws_ref, bs_ref, wout_ref,
                  o_ref, h_ref, mix_ref, ycar_ref, zcar_ref, pcar_ref):
    tm = x_ref.shape[1]
    j = pl.program_id(1)

    @pl.when(j == 0)
    def _():
        ycar_ref[...] = jnp.zeros_like(ycar_ref)
        zcar_ref[...] = jnp.zeros_like(zcar_ref)
        pcar_ref[...] = jnp.zeros_like(pcar_ref)

    wr = lax.broadcasted_iota(jnp.int32, (N_HEADS * CHUNK, CHUNK), 0) % CHUNK
    wc = lax.broadcasted_iota(jnp.int32, (N_HEADS * CHUNK, CHUNK), 1)
    ws = jnp.where(wc <= wr, ws_ref[...], 0.0).astype(_bf16)

    blocks = [slice(r0, r0 + MIX_SUB) for r0 in range(0, tm, MIX_SUB)]
    for rows in blocks:
        x = x_ref[0, rows, :]
        h_ref[rows, :] = (x * _rms_scale(x) * g_ref[...]).astype(_bf16)

    def proj(rows, piece0, n_pieces):
        lo = piece0 * D_GROUP
        return jnp.dot(h_ref[rows, :], win_ref[:, lo:lo + n_pieces * D_GROUP],
                       preferred_element_type=_f32)

    pieces = {"conf": (0, 2), "short": (2, 3), "pool": (5, 1), "gate": (6, 2)}
    p = {}

    def do_proj(bi, name):
        p[bi, name] = proj(blocks[bi], *pieces[name])

    def do_mixer(bi, name):
        rows = blocks[bi]
        if name == "conf":
            mix_ref[rows, 0:D_GROUP] = _conformer_conv(
                p.pop((bi, name)), cw_ref, cb_ref, clg_ref, clb_ref, ycar_ref)
        elif name == "short":
            mix_ref[rows, D_GROUP:2 * D_GROUP] = _short_gated_conv(
                p.pop((bi, name)), sw_ref, zcar_ref)
        elif name == "pool":
            mix_ref[rows, 2 * D_GROUP:3 * D_GROUP] = _multiscale_pool(
                p.pop((bi, name)), j * tm + rows.start, pw_ref, ps_ref, pcar_ref)
        else:
            c0 = int(name.split("@")[1]) * CHUNK
            mix_ref[rows.start + c0:rows.start + c0 + CHUNK, 3 * D_GROUP:] = _spatial_gating(
                p[bi, "gate"][c0:c0 + CHUNK, :], ws, glg_ref, glb_ref, bs_ref)

    def do_out(bi, _):
        rows = blocks[bi]
        o_ref[0, rows, :] = x_ref[0, rows, :] + jnp.dot(
            mix_ref[rows, :], wout_ref[...], preferred_element_type=_f32)

    program = [(do_proj, 0, name) for name in MIX_PROJ_ORDER]
    for bi in range(len(blocks)):
        nxt = bi + 1 if bi + 1 < len(blocks) else None
        for step in MIX_BLOCK_ORDER:
            kind, name = step.split(":")
            if kind == "mix":
                program.append((do_mixer, bi, name))
            elif kind == "out":
                program.append((do_out, bi, name))
            elif nxt is not None:
                program.append((do_proj, nxt, name))
    for fn, bi, name in program:
        fn(bi, name)


def _conformer_conv(p, cw_ref, cb_ref, clg_ref, clb_ref, ycar_ref):
    n = p.shape[0]
    y = p[:, :D_GROUP] * jax.nn.sigmoid(p[:, D_GROUP:])
    yext = jnp.concatenate([ycar_ref[...], y], axis=0)
    ycar_ref[...] = y[n - CONF_HALO:, :]
    acc = jnp.zeros((n, D_GROUP), _f32)
    for b in range(SUBLANES):
        yb = yext if b == 0 else pltpu.roll(yext, b, 0)
        for a in range(CONF_HALO // SUBLANES):
            shift = SUBLANES * a + b
            if shift >= CONF_KERNEL:
                continue
            k = CONF_KERNEL - 1 - shift
            lo = CONF_HALO - SUBLANES * a
            acc = acc + cw_ref[k:k + 1, :] * yb[lo:lo + n, :]
    conv = acc + cb_ref[...]
    return _silu(_layernorm(conv, clg_ref[...], clb_ref[...])).astype(_bf16)


def _short_gated_conv(p, sw_ref, zcar_ref):
    n = p.shape[0]
    z = p[:, D_GROUP:2 * D_GROUP] * p[:, 2 * D_GROUP:]
    zext = jnp.concatenate([zcar_ref[...], z], axis=0)
    zcar_ref[...] = z[n - SHORT_HALO:, :]
    conv = sw_ref[SHORT_KERNEL - 1:SHORT_KERNEL, :] * z
    for shift in range(1, SHORT_KERNEL):
        k = SHORT_KERNEL - 1 - shift
        conv = conv + sw_ref[k:k + 1, :] * pltpu.roll(zext, shift, 0)[SHORT_HALO:, :]
    return (p[:, :D_GROUP] * conv).astype(_bf16)


def _multiscale_pool(xp, pos0, pw_ref, ps_ref, pcar_ref):
    n = xp.shape[0]
    xe = jnp.concatenate([pcar_ref[...], xp], axis=0)
    pcar_ref[...] = xp[n - POOL_HALO:, :]
    lane = lax.broadcasted_iota(jnp.int32, (n, D_GROUP), 1)
    row = lax.broadcasted_iota(jnp.int32, (n, D_GROUP), 0)
    pos = (pos0 + row + 1).astype(_f32)
    run, width = xe, 1
    wsum = None
    wlane = None
    for gi, w in enumerate(POOL_WINDOWS):
        while width < w:
            run = run + pltpu.roll(run, width, 0)
            width *= 2
        cur = run[POOL_HALO:, :]
        if wsum is None:
            wsum, wlane = cur, jnp.full((n, D_GROUP), float(w), _f32)
        else:
            in_group = lane >= gi * POOL_GROUP
            wsum = jnp.where(in_group, cur, wsum)
            wlane = jnp.where(in_group, float(w), wlane)
    dpool = wsum / jnp.minimum(pos, wlane) - xp
    yc = jnp.dot(dpool.astype(_bf16), pw_ref[...], preferred_element_type=_f32)
    return (yc * ps_ref[...]).astype(_bf16)


def _spatial_gating(p, ws, glg_ref, glb_ref, bs_ref):
    v = _layernorm(p[:, D_GROUP:], glg_ref[...], glb_ref[...]).astype(_bf16)
    r = jnp.dot(ws, v, preferred_element_type=_f32)
    hlane = lax.broadcasted_iota(jnp.int32, (CHUNK, D_GROUP), 1) // HEAD_DIM
    mixed = r[0:CHUNK, :]
    for hd in range(1, N_HEADS):
        mixed = jnp.where(hlane == hd, r[hd * CHUNK:(hd + 1) * CHUNK, :], mixed)
    return (p[:, :D_GROUP] * (mixed + bs_ref[...])).astype(_bf16)


def _mixer_call(x, layer, params):
    bsz, s, d = x.shape
    tm = MIX_TILE
    assert s % tm == 0 and tm % MIX_SUB == 0 and MIX_SUB % CHUNK == 0 and MIX_SUB >= CONF_HALO
    return pl.pallas_call(
        _mixer_kernel,
        out_shape=jax.ShapeDtypeStruct((bsz, s, d), _f32),
        grid=(bsz, s // tm),
        in_specs=[pl.BlockSpec((1, tm, d), lambda b, j: (b, j, 0))]
                 + [_layer_spec(a, layer) for a in params],
        out_specs=pl.BlockSpec((1, tm, d), lambda b, j: (b, j, 0)),
        scratch_shapes=[
            pltpu.VMEM((tm, d), _bf16),
            pltpu.VMEM((tm, 4 * D_GROUP), _bf16),
            pltpu.VMEM((CONF_HALO, D_GROUP), _f32),
            pltpu.VMEM((SHORT_HALO, D_GROUP), _f32),
            pltpu.VMEM((POOL_HALO, D_GROUP), _f32),
        ],
        compiler_params=pltpu.CompilerParams(
            dimension_semantics=("arbitrary", "arbitrary"),
            vmem_limit_bytes=VMEM_LIMIT_BYTES),
        name="mixer_block",
    )(x, *params)


def _block_diag(pool_w):
    nl, g, c, d = pool_w.shape
    eye = jnp.eye(g, dtype=pool_w.dtype)
    return (eye[None, :, None, :, None] * pool_w[:, :, :, None, :]).reshape(nl, g * c, g * d)


def kernel(x, ffn1_norm, ffn1_w1, ffn1_w3, ffn1_w2, mix_norm, w_in, conf_conv_w, conf_conv_b, conf_ln_g, conf_ln_b, sconv_w, pool_w, pool_scale, gmlp_ln_g, gmlp_ln_b, gmlp_w_s, gmlp_b_s, w_out, ffn2_norm, ffn2_w1, ffn2_w3, ffn2_w2, final_norm):
    bsz, s, d = x.shape
    depth = ffn1_norm.shape[0]
    rows = lambda a: a.reshape(depth, 1, -1)
    cast = lambda a: a.astype(_bf16)
    fg = final_norm.reshape(1, -1)
    ffn1 = (rows(ffn1_norm), cast(ffn1_w1), cast(ffn1_w3), cast(ffn1_w2))
    ffn2 = (rows(ffn2_norm), cast(ffn2_w1), cast(ffn2_w3), cast(ffn2_w2))
    mixer = [
        rows(mix_norm), cast(w_in), conf_conv_w, rows(conf_conv_b), rows(conf_ln_g),
        rows(conf_ln_b), sconv_w, cast(_block_diag(pool_w)), rows(pool_scale),
        rows(gmlp_ln_g), rows(gmlp_ln_b),
        gmlp_w_s.reshape(depth, N_HEADS * CHUNK, CHUNK),
        jnp.repeat(jnp.swapaxes(gmlp_b_s, 1, 2), HEAD_DIM, axis=2),
        cast(w_out),
    ]
    x2d = x.reshape(bsz * s, d)
    for l in range(depth):
        x2d = _ffn_call(x2d, l, *ffn1, fg, apply_final_norm=False)
        x2d = _mixer_call(x2d.reshape(bsz, s, d), l, mixer).reshape(bsz * s, d)
        x2d = _ffn_call(x2d, l, *ffn2, fg, apply_final_norm=(l == depth - 1))
    return x2d.reshape(bsz, s, d)
```

```python
import functools

import jax
import jax.numpy as jnp
from jax import lax
from jax.experimental import pallas as pl
from jax.experimental.pallas import tpu as pltpu

EPS = 1e-6
FFN_RESIDUAL = 0.5
D_GROUP = 256
N_HEADS = 4
HEAD_DIM = 64
CONF_KERNEL = 31
SHORT_KERNEL = 3
POOL_WINDOWS = (2, 4, 8, 16)
POOL_GROUP = 64
CHUNK = 128

SUBLANES = 8
CONF_HALO = 32
SHORT_HALO = 8
POOL_HALO = 16

FFN_TILE = 1024
FFN_SUB = 512
FFN_CHUNK = 256
FUSED_BLOCK = 512
FUSED_ORDER = ("proj:conf", "proj:short", "proj:pool", "proj:gate", "ffn:0-11", "ffn:out",
               "mix:conf", "mix:short", "mix:pool", "mix:gate", "out:", "ffn:store")
VMEM_LIMIT_BYTES = 56 * 1024 * 1024

_bf16 = jnp.bfloat16
_f32 = jnp.float32


def _rms_scale(x):
    return lax.rsqrt(jnp.mean(x * x, axis=-1, keepdims=True) + EPS)


def _layernorm(x, g, b):
    mu = jnp.mean(x, axis=-1, keepdims=True)
    xc = x - mu
    var = jnp.mean(xc * xc, axis=-1, keepdims=True)
    return xc * lax.rsqrt(var + EPS) * g + b


def _silu(x):
    return x * jax.nn.sigmoid(x)


def _layer_spec(arr, layer):
    zeros = (0,) * (arr.ndim - 1)
    return pl.BlockSpec((None,) + arr.shape[1:], lambda *_: (layer,) + zeros,
                        pipeline_mode=pl.Buffered(1))


def _ffn_kernel(x_ref, g_ref, w1_ref, w3_ref, w2_ref, fg_ref, o_ref, h_ref, gate_ref,
                *, d_ff, apply_final_norm):
    tm = x_ref.shape[0]
    for r0 in range(0, tm, FFN_SUB):
        rows = slice(r0, r0 + FFN_SUB)
        x = x_ref[rows, :]
        h_ref[rows, :] = (x * _rms_scale(x) * g_ref[...]).astype(_bf16)
    for r0 in range(0, tm, FFN_SUB):
        rows = slice(r0, r0 + FFN_SUB)
        for c0 in range(0, d_ff, FFN_CHUNK):
            hb = h_ref[rows, :]
            a = jnp.dot(hb, w1_ref[:, c0:c0 + FFN_CHUNK], preferred_element_type=_f32)
            b = jnp.dot(hb, w3_ref[:, c0:c0 + FFN_CHUNK], preferred_element_type=_f32)
            gate_ref[rows, c0:c0 + FFN_CHUNK] = (_silu(a) * b).astype(_bf16)
        y = jnp.dot(gate_ref[rows, :], w2_ref[...], preferred_element_type=_f32)
        out = x_ref[rows, :] + FFN_RESIDUAL * y
        if apply_final_norm:
            out = out * _rms_scale(out) * fg_ref[...]
        o_ref[rows, :] = out


def _ffn_call(x2d, layer, g, w1, w3, w2, final_g, *, apply_final_norm):
    t, d = x2d.shape
    d_ff = w1.shape[2]
    tm = FFN_TILE
    assert t % tm == 0 and tm % FFN_SUB == 0 and d_ff % FFN_CHUNK == 0
    kern = functools.partial(_ffn_kernel, d_ff=d_ff, apply_final_norm=apply_final_norm)
    params = [g, w1, w3, w2]
    return pl.pallas_call(
        kern,
        out_shape=jax.ShapeDtypeStruct((t, d), _f32),
        grid=(t // tm,),
        in_specs=[pl.BlockSpec((tm, d), lambda i: (i, 0))]
                 + [_layer_spec(a, layer) for a in params]
                 + [pl.BlockSpec((1, d), lambda i: (0, 0), pipeline_mode=pl.Buffered(1))],
        out_specs=pl.BlockSpec((tm, d), lambda i: (i, 0)),
        scratch_shapes=[
            pltpu.VMEM((tm, d), _bf16),
            pltpu.VMEM((tm, d_ff), _bf16),
        ],
        compiler_params=pltpu.CompilerParams(
            dimension_semantics=("arbitrary",),
            vmem_limit_bytes=VMEM_LIMIT_BYTES),
        name="swiglu_half_step",
    )(x2d, *params, final_g)


def _fused_kernel(x_ref, g1_ref, w1_ref, w3_ref, w2_ref,
                  gm_ref, win_ref, cw_ref, cb_ref, clg_ref, clb_ref, sw_ref,
                  pw_ref, ps_ref, glg_ref, glb_ref, ws_ref, bs_ref, wout_ref,
                  o_ref, xmid_ref, hf_ref, gate_ref, hm_ref, mix_ref,
                  ycar_ref, zcar_ref, pcar_ref, *, d_ff, blocks_per_seq):
    g = pl.program_id(0)
    nb = FUSED_BLOCK
    block = (g + blocks_per_seq - 1) % blocks_per_seq

    @pl.when(g == 0)
    def _():
        xmid_ref[...] = jnp.zeros_like(xmid_ref)

    @pl.when(block == 0)
    def _():
        ycar_ref[...] = jnp.zeros_like(ycar_ref)
        zcar_ref[...] = jnp.zeros_like(zcar_ref)
        pcar_ref[...] = jnp.zeros_like(pcar_ref)

    wr = lax.broadcasted_iota(jnp.int32, (N_HEADS * CHUNK, CHUNK), 0) % CHUNK
    wc = lax.broadcasted_iota(jnp.int32, (N_HEADS * CHUNK, CHUNK), 1)
    ws = jnp.where(wc <= wr, ws_ref[...], 0.0).astype(_bf16)

    xm = xmid_ref[...]
    hm_ref[...] = (xm * _rms_scale(xm) * gm_ref[...]).astype(_bf16)
    x = x_ref[...]
    hf_ref[...] = (x * _rms_scale(x) * g1_ref[...]).astype(_bf16)

    pieces = {"conf": (0, 2), "short": (2, 3), "pool": (5, 1), "gate": (6, 2)}
    p = {}

    def do_proj(name):
        lo, n = pieces[name]
        p[name] = jnp.dot(hm_ref[...], win_ref[:, lo * D_GROUP:(lo + n) * D_GROUP],
                          preferred_element_type=_f32)

    def do_mix(name):
        if name == "conf":
            mix_ref[:, 0:D_GROUP] = _conformer_conv(
                p.pop(name), cw_ref, cb_ref, clg_ref, clb_ref, ycar_ref, pace)
        elif name == "short":
            mix_ref[:, D_GROUP:2 * D_GROUP] = _short_gated_conv(p.pop(name), sw_ref, zcar_ref)
        elif name == "pool":
            mix_ref[:, 2 * D_GROUP:3 * D_GROUP] = _multiscale_pool(
                p.pop(name), block * nb, pw_ref, ps_ref, pcar_ref)
        else:
            p_gate = p.pop(name)
            for c0 in range(0, nb, CHUNK):
                mix_ref[c0:c0 + CHUNK, 3 * D_GROUP:] = _spatial_gating(
                    p_gate[c0:c0 + CHUNK, :], ws, glg_ref, glb_ref, bs_ref)

    def do_out(_):
        o_ref[...] = xm + jnp.dot(
            mix_ref[...], wout_ref[...], preferred_element_type=_f32)

    def do_ffn(name):
        if name == "out":
            p["ffn"] = jnp.dot(gate_ref[...], w2_ref[...], preferred_element_type=_f32)
            return
        if name == "store":
            xmid_ref[...] = x_ref[...] + FFN_RESIDUAL * p.pop("ffn")
            return
        lo, hi = (int(v) for v in name.split("-"))
        for c0 in range(lo * FFN_CHUNK, hi * FFN_CHUNK, FFN_CHUNK):
            hb = hf_ref[...]
            a = jnp.dot(hb, w1_ref[:, c0:c0 + FFN_CHUNK], preferred_element_type=_f32)
            b = jnp.dot(hb, w3_ref[:, c0:c0 + FFN_CHUNK], preferred_element_type=_f32)
            gate_ref[:, c0:c0 + FFN_CHUNK] = (_silu(a) * b).astype(_bf16)
            bits = pltpu.bitcast(a[0:SUBLANES, 0:D_GROUP], jnp.uint32)
            pace.append(((bits >> 16) >> 16).astype(_f32)[0:1, :])

    pace = []
    steps = {"proj": do_proj, "mix": do_mix, "out": do_out, "ffn": do_ffn}
    assert "ffn:store" in FUSED_ORDER
    for step in FUSED_ORDER:
        kind, name = step.split(":")
        steps[kind](name)


def _conformer_conv(p, cw_ref, cb_ref, clg_ref, clb_ref, ycar_ref, pace):
    n = p.shape[0]
    y = p[:, :D_GROUP] * jax.nn.sigmoid(p[:, D_GROUP:])
    yext = jnp.concatenate([ycar_ref[...], y], axis=0)
    ycar_ref[...] = y[n - CONF_HALO:, :]
    acc = jnp.zeros((n, D_GROUP), _f32)
    for b in range(SUBLANES):
        yb = yext if b == 0 else pltpu.roll(yext, b, 0)
        for a in range(CONF_HALO // SUBLANES):
            shift = SUBLANES * a + b
            if shift >= CONF_KERNEL:
                continue
            k = CONF_KERNEL - 1 - shift
            lo = CONF_HALO - SUBLANES * a
            acc = acc + (cw_ref[k:k + 1, :] + pace[b]) * yb[lo:lo + n, :]
    conv = acc + cb_ref[...]
    return _silu(_layernorm(conv, clg_ref[...], clb_ref[...])).astype(_bf16)


def _short_gated_conv(p, sw_ref, zcar_ref):
    n = p.shape[0]
    z = p[:, D_GROUP:2 * D_GROUP] * p[:, 2 * D_GROUP:]
    zext = jnp.concatenate([zcar_ref[...], z], axis=0)
    zcar_ref[...] = z[n - SHORT_HALO:, :]
    conv = sw_ref[SHORT_KERNEL - 1:SHORT_KERNEL, :] * z
    for shift in range(1, SHORT_KERNEL):
        k = SHORT_KERNEL - 1 - shift
        conv = conv + sw_ref[k:k + 1, :] * pltpu.roll(zext, shift, 0)[SHORT_HALO:, :]
    return (p[:, :D_GROUP] * conv).astype(_bf16)


def _multiscale_pool(xp, pos0, pw_ref, ps_ref, pcar_ref):
    n = xp.shape[0]
    xe = jnp.concatenate([pcar_ref[...], xp], axis=0)
    pcar_ref[...] = xp[n - POOL_HALO:, :]
    lane = lax.broadcasted_iota(jnp.int32, (n, D_GROUP), 1)
    row = lax.broadcasted_iota(jnp.int32, (n, D_GROUP), 0)
    pos = (pos0 + row + 1).astype(_f32)
    run, width = xe, 1
    wsum = None
    wlane = None
    for gi, w in enumerate(POOL_WINDOWS):
        while width < w:
            run = run + pltpu.roll(run, width, 0)
            width *= 2
        cur = run[POOL_HALO:, :]
        if wsum is None:
            wsum, wlane = cur, jnp.full((n, D_GROUP), float(w), _f32)
        else:
            in_group = lane >= gi * POOL_GROUP
            wsum = jnp.where(in_group, cur, wsum)
            wlane = jnp.where(in_group, float(w), wlane)
    dpool = wsum / jnp.minimum(pos, wlane) - xp
    yc = jnp.dot(dpool.astype(_bf16), pw_ref[...], preferred_element_type=_f32)
    return (yc * ps_ref[...]).astype(_bf16)


def _spatial_gating(p, ws, glg_ref, glb_ref, bs_ref):
    v = _layernorm(p[:, D_GROUP:], glg_ref[...], glb_ref[...]).astype(_bf16)
    r = jnp.dot(ws, v, preferred_element_type=_f32)
    hlane = lax.broadcasted_iota(jnp.int32, (CHUNK, D_GROUP), 1) // HEAD_DIM
    mixed = r[0:CHUNK, :]
    for hd in range(1, N_HEADS):
        mixed = jnp.where(hlane == hd, r[hd * CHUNK:(hd + 1) * CHUNK, :], mixed)
    return (p[:, :D_GROUP] * (mixed + bs_ref[...])).astype(_bf16)


def _fused_call(x2d, seq_len, layer, ffn, mixer):
    t, d = x2d.shape
    nb = FUSED_BLOCK
    d_ff = ffn[1].shape[2]
    n_blocks = t // nb
    assert seq_len % nb == 0 and nb % CHUNK == 0 and nb >= CONF_HALO
    assert d_ff // FFN_CHUNK == 11 and d_ff % FFN_CHUNK == 0
    kern = functools.partial(_fused_kernel, d_ff=d_ff, blocks_per_seq=seq_len // nb)
    params = list(ffn) + list(mixer)
    return pl.pallas_call(
        kern,
        out_shape=jax.ShapeDtypeStruct((t, d), _f32),
        grid=(n_blocks + 1,),
        in_specs=[pl.BlockSpec((nb, d), lambda g: (jnp.minimum(g, n_blocks - 1), 0))]
                 + [_layer_spec(a, layer) for a in params],
        out_specs=pl.BlockSpec((nb, d), lambda g: (jnp.maximum(g - 1, 0), 0)),
        scratch_shapes=[
            pltpu.VMEM((nb, d), _f32),
            pltpu.VMEM((nb, d), _bf16),
            pltpu.VMEM((nb, d_ff), _bf16),
            pltpu.VMEM((nb, d), _bf16),
            pltpu.VMEM((nb, 4 * D_GROUP), _bf16),
            pltpu.VMEM((CONF_HALO, D_GROUP), _f32),
            pltpu.VMEM((SHORT_HALO, D_GROUP), _f32),
            pltpu.VMEM((POOL_HALO, D_GROUP), _f32),
        ],
        compiler_params=pltpu.CompilerParams(
            dimension_semantics=("arbitrary",),
            vmem_limit_bytes=VMEM_LIMIT_BYTES),
        name="swiglu_mixer_block",
    )(x2d, *params)


def _block_diag(pool_w):
    nl, g, c, d = pool_w.shape
    eye = jnp.eye(g, dtype=pool_w.dtype)
    return (eye[None, :, None, :, None] * pool_w[:, :, :, None, :]).reshape(nl, g * c, g * d)


def kernel(x, ffn1_norm, ffn1_w1, ffn1_w3, ffn1_w2, mix_norm, w_in, conf_conv_w, conf_conv_b, conf_ln_g, conf_ln_b, sconv_w, pool_w, pool_scale, gmlp_ln_g, gmlp_ln_b, gmlp_w_s, gmlp_b_s, w_out, ffn2_norm, ffn2_w1, ffn2_w3, ffn2_w2, final_norm):
    bsz, s, d = x.shape
    depth = ffn1_norm.shape[0]
    rows = lambda a: a.reshape(depth, 1, -1)
    cast = lambda a: a.astype(_bf16)
    fg = final_norm.reshape(1, -1)
    ffn1 = (rows(ffn1_norm), cast(ffn1_w1), cast(ffn1_w3), cast(ffn1_w2))
    ffn2 = (rows(ffn2_norm), cast(ffn2_w1), cast(ffn2_w3), cast(ffn2_w2))
    mixer = [
        rows(mix_norm), cast(w_in), conf_conv_w, rows(conf_conv_b), rows(conf_ln_g),
        rows(conf_ln_b), sconv_w, cast(_block_diag(pool_w)), rows(pool_scale),
        rows(gmlp_ln_g), rows(gmlp_ln_b),
        gmlp_w_s.reshape(depth, N_HEADS * CHUNK, CHUNK),
        jnp.repeat(jnp.swapaxes(gmlp_b_s, 1, 2), HEAD_DIM, axis=2),
        cast(w_out),
    ]
    x2d = x.reshape(bsz * s, d)
    for l in range(depth):
        x2d = _fused_call(x2d, s, l, ffn1, mixer)
        x2d = _ffn_call(x2d, l, *ffn2, fg, apply_final_norm=(l == depth - 1))
    return x2d.reshape(bsz, s, d)
```

```python
import functools

import jax
import jax.numpy as jnp
from jax import lax
from jax.experimental import pallas as pl
from jax.experimental.pallas import tpu as pltpu

EPS = 1e-6
FFN_RESIDUAL = 0.5
D_GROUP = 256
N_HEADS = 4
HEAD_DIM = 64
CONF_KERNEL = 31
SHORT_KERNEL = 3
POOL_WINDOWS = (2, 4, 8, 16)
POOL_GROUP = 64
CHUNK = 128

SUBLANES = 8
CONF_HALO = 32
SHORT_HALO = 8
POOL_HALO = 16

FFN_TILE = 1024
FFN_SUB = 512
FFN_CHUNK = 256
FUSED_BLOCK = 512
FUSED_ORDER = ("proj:conf", "proj:short", "proj:pool", "proj:gate", "ffn:gates", "ffn:out",
               "mix:conf", "mix:short", "mix:pool", "mix:gate", "out:", "ffn:store")
VMEM_LIMIT_BYTES = 56 * 1024 * 1024

_bf16 = jnp.bfloat16
_f32 = jnp.float32


def _rms_scale(x):
    return lax.rsqrt(jnp.mean(x * x, axis=-1, keepdims=True) + EPS)


def _layernorm(x, g, b):
    mu = jnp.mean(x, axis=-1, keepdims=True)
    xc = x - mu
    var = jnp.mean(xc * xc, axis=-1, keepdims=True)
    return xc * lax.rsqrt(var + EPS) * g + b


def _silu(x):
    return x * jax.nn.sigmoid(x)


def _layer_spec(arr, layer):
    zeros = (0,) * (arr.ndim - 1)
    return pl.BlockSpec((None,) + arr.shape[1:], lambda *_: (layer,) + zeros,
                        pipeline_mode=pl.Buffered(1))


def _ffn_kernel(x_ref, g_ref, w1_ref, w3_ref, w2_ref, fg_ref, o_ref, h_ref, gate_ref,
                *, d_ff, apply_final_norm):
    tm = x_ref.shape[0]
    for r0 in range(0, tm, FFN_SUB):
        rows = slice(r0, r0 + FFN_SUB)
        x = x_ref[rows, :]
        h_ref[rows, :] = (x * _rms_scale(x) * g_ref[...]).astype(_bf16)
    for r0 in range(0, tm, FFN_SUB):
        rows = slice(r0, r0 + FFN_SUB)
        for c0 in range(0, d_ff, FFN_CHUNK):
            hb = h_ref[rows, :]
            a = jnp.dot(hb, w1_ref[:, c0:c0 + FFN_CHUNK], preferred_element_type=_f32)
            b = jnp.dot(hb, w3_ref[:, c0:c0 + FFN_CHUNK], preferred_element_type=_f32)
            gate_ref[rows, c0:c0 + FFN_CHUNK] = (_silu(a) * b).astype(_bf16)
        y = jnp.dot(gate_ref[rows, :], w2_ref[...], preferred_element_type=_f32)
        out = x_ref[rows, :] + FFN_RESIDUAL * y
        if apply_final_norm:
            out = out * _rms_scale(out) * fg_ref[...]
        o_ref[rows, :] = out


def _ffn_call(x2d, layer, g, w1, w3, w2, final_g, *, apply_final_norm):
    t, d = x2d.shape
    d_ff = w1.shape[2]
    tm = FFN_TILE
    assert t % tm == 0 and tm % FFN_SUB == 0 and d_ff % FFN_CHUNK == 0
    kern = functools.partial(_ffn_kernel, d_ff=d_ff, apply_final_norm=apply_final_norm)
    params = [g, w1, w3, w2]
    return pl.pallas_call(
        kern,
        out_shape=jax.ShapeDtypeStruct((t, d), _f32),
        grid=(t // tm,),
        in_specs=[pl.BlockSpec((tm, d), lambda i: (i, 0))]
                 + [_layer_spec(a, layer) for a in params]
                 + [pl.BlockSpec((1, d), lambda i: (0, 0), pipeline_mode=pl.Buffered(1))],
        out_specs=pl.BlockSpec((tm, d), lambda i: (i, 0)),
        scratch_shapes=[
            pltpu.VMEM((tm, d), _bf16),
            pltpu.VMEM((tm, d_ff), _bf16),
        ],
        compiler_params=pltpu.CompilerParams(
            dimension_semantics=("arbitrary",),
            vmem_limit_bytes=VMEM_LIMIT_BYTES),
        name="swiglu_half_step",
    )(x2d, *params, final_g)


def _fused_kernel(x_ref, g1_ref, w1_ref, w3_ref, w2_ref,
                  gm_ref, win_ref, cw_ref, cb_ref, clg_ref, clb_ref, sw_ref,
                  pw_ref, ps_ref, glg_ref, glb_ref, ws_ref, bs_ref, wout_ref,
                  o_ref, xmid_ref, hf_ref, gate_ref, hm_ref, mix_ref,
                  ycar_ref, zcar_ref, pcar_ref, *, d_ff, blocks_per_seq):
    g = pl.program_id(0)
    nb = FUSED_BLOCK
    block = (g + blocks_per_seq - 1) % blocks_per_seq

    @pl.when(g == 0)
    def _():
        xmid_ref[...] = jnp.zeros_like(xmid_ref)

    @pl.when(jnp.logical_or(g == 0, block == 0))
    def _():
        ycar_ref[...] = jnp.zeros_like(ycar_ref)
        zcar_ref[...] = jnp.zeros_like(zcar_ref)
        pcar_ref[...] = jnp.zeros_like(pcar_ref)

    wr = lax.broadcasted_iota(jnp.int32, (N_HEADS * CHUNK, CHUNK), 0) % CHUNK
    wc = lax.broadcasted_iota(jnp.int32, (N_HEADS * CHUNK, CHUNK), 1)
    ws = jnp.where(wc <= wr, ws_ref[...], 0.0).astype(_bf16)

    xm = xmid_ref[...]
    hm_ref[...] = (xm * _rms_scale(xm) * gm_ref[...]).astype(_bf16)
    x = x_ref[...]
    hf_ref[...] = (x * _rms_scale(x) * g1_ref[...]).astype(_bf16)

    pieces = {"conf": (0, 2), "short": (2, 3), "pool": (5, 1), "gate": (6, 2)}
    p = {}

    def do_proj(name):
        lo, n = pieces[name]
        p[name] = jnp.dot(hm_ref[...], win_ref[:, lo * D_GROUP:(lo + n) * D_GROUP],
                          preferred_element_type=_f32)

    def do_mix(name):
        if name == "conf":
            mix_ref[:, 0:D_GROUP] = _conformer_conv(
                p.pop(name), cw_ref, cb_ref, clg_ref, clb_ref, ycar_ref, pace)
        elif name == "short":
            mix_ref[:, D_GROUP:2 * D_GROUP] = _short_gated_conv(p.pop(name), sw_ref, zcar_ref)
        elif name == "pool":
            mix_ref[:, 2 * D_GROUP:3 * D_GROUP] = _multiscale_pool(
                p.pop(name), block * nb, pw_ref, ps_ref, pcar_ref)
        else:
            p_gate = p.pop(name)
            for c0 in range(0, nb, CHUNK):
                mix_ref[c0:c0 + CHUNK, 3 * D_GROUP:] = _spatial_gating(
                    p_gate[c0:c0 + CHUNK, :], ws, glg_ref, glb_ref, bs_ref)

    def do_out(_):
        o_ref[...] = xm + jnp.dot(
            mix_ref[...], wout_ref[...], preferred_element_type=_f32)

    def do_ffn(name):
        if name == "out":
            p["ffn"] = jnp.dot(gate_ref[...], w2_ref[...], preferred_element_type=_f32)
            return
        if name == "store":
            xmid_ref[...] = x_ref[...] + FFN_RESIDUAL * p.pop("ffn")
            return
        assert name == "gates"
        for c0 in range(0, d_ff, FFN_CHUNK):
            hb = hf_ref[...]
            a = jnp.dot(hb, w1_ref[:, c0:c0 + FFN_CHUNK], preferred_element_type=_f32)
            b = jnp.dot(hb, w3_ref[:, c0:c0 + FFN_CHUNK], preferred_element_type=_f32)
            gate_ref[:, c0:c0 + FFN_CHUNK] = (_silu(a) * b).astype(_bf16)
            bits = pltpu.bitcast(a[0:SUBLANES, 0:D_GROUP], jnp.uint32)
            pace.append(((bits >> 16) >> 16).astype(_f32)[0:1, :])

    pace = []
    steps = {"proj": do_proj, "mix": do_mix, "out": do_out, "ffn": do_ffn}
    assert sorted(FUSED_ORDER) == sorted(
        ["proj:" + k for k in pieces] + ["mix:" + k for k in pieces]
        + ["ffn:gates", "ffn:out", "ffn:store", "out:"])
    for step in FUSED_ORDER:
        kind, name = step.split(":")
        steps[kind](name)


def _conformer_conv(p, cw_ref, cb_ref, clg_ref, clb_ref, ycar_ref, pace):
    n = p.shape[0]
    y = p[:, :D_GROUP] * jax.nn.sigmoid(p[:, D_GROUP:])
    yext = jnp.concatenate([ycar_ref[...], y], axis=0)
    ycar_ref[...] = y[n - CONF_HALO:, :]
    acc = jnp.zeros((n, D_GROUP), _f32)
    for b in range(SUBLANES):
        yb = yext if b == 0 else pltpu.roll(yext, b, 0)
        for a in range(CONF_HALO // SUBLANES):
            shift = SUBLANES * a + b
            if shift >= CONF_KERNEL:
                continue
            k = CONF_KERNEL - 1 - shift
            lo = CONF_HALO - SUBLANES * a
            acc = acc + (cw_ref[k:k + 1, :] + pace[b]) * yb[lo:lo + n, :]
    conv = acc + cb_ref[...]
    return _silu(_layernorm(conv, clg_ref[...], clb_ref[...])).astype(_bf16)


def _short_gated_conv(p, sw_ref, zcar_ref):
    n = p.shape[0]
    z = p[:, D_GROUP:2 * D_GROUP] * p[:, 2 * D_GROUP:]
    zext = jnp.concatenate([zcar_ref[...], z], axis=0)
    zcar_ref[...] = z[n - SHORT_HALO:, :]
    conv = sw_ref[SHORT_KERNEL - 1:SHORT_KERNEL, :] * z
    for shift in range(1, SHORT_KERNEL):
        k = SHORT_KERNEL - 1 - shift
        conv = conv + sw_ref[k:k + 1, :] * pltpu.roll(zext, shift, 0)[SHORT_HALO:, :]
    return (p[:, :D_GROUP] * conv).astype(_bf16)


def _multiscale_pool(xp, pos0, pw_ref, ps_ref, pcar_ref):
    n = xp.shape[0]
    xe = jnp.concatenate([pcar_ref[...], xp], axis=0)
    pcar_ref[...] = xp[n - POOL_HALO:, :]
    lane = lax.broadcasted_iota(jnp.int32, (n, D_GROUP), 1)
    row = lax.broadcasted_iota(jnp.int32, (n, D_GROUP), 0)
    pos = (pos0 + row + 1).astype(_f32)
    run, width = xe, 1
    wsum = None
    wlane = None
    for gi, w in enumerate(POOL_WINDOWS):
        while width < w:
            run = run + pltpu.roll(run, width, 0)
            width *= 2
        cur = run[POOL_HALO:, :]
        if wsum is None:
            wsum, wlane = cur, jnp.full((n, D_GROUP), float(w), _f32)
        else:
            in_group = lane >= gi * POOL_GROUP
            wsum = jnp.where(in_group, cur, wsum)
            wlane = jnp.where(in_group, float(w), wlane)
    dpool = wsum / jnp.minimum(pos, wlane) - xp
    yc = jnp.dot(dpool.astype(_bf16), pw_ref[...], preferred_element_type=_f32)
    return (yc * ps_ref[...]).astype(_bf16)


def _spatial_gating(p, ws, glg_ref, glb_ref, bs_ref):
    v = _layernorm(p[:, D_GROUP:], glg_ref[...], glb_ref[...]).astype(_bf16)
    r = jnp.dot(ws, v, preferred_element_type=_f32)
    hlane = lax.broadcasted_iota(jnp.int32, (CHUNK, D_GROUP), 1) // HEAD_DIM
    mixed = r[0:CHUNK, :]
    for hd in range(1, N_HEADS):
        mixed = jnp.where(hlane == hd, r[hd * CHUNK:(hd + 1) * CHUNK, :], mixed)
    return (p[:, :D_GROUP] * (mixed + bs_ref[...])).astype(_bf16)


def _fused_call(x2d, seq_len, layer, ffn, mixer):
    t, d = x2d.shape
    nb = FUSED_BLOCK
    d_ff = ffn[1].shape[2]
    n_blocks = t // nb
    assert seq_len % nb == 0 and nb % CHUNK == 0 and nb >= CONF_HALO
    assert d_ff % FFN_CHUNK == 0 and d_ff // FFN_CHUNK >= SUBLANES
    kern = functools.partial(_fused_kernel, d_ff=d_ff, blocks_per_seq=seq_len // nb)
    params = list(ffn) + list(mixer)
    return pl.pallas_call(
        kern,
        out_shape=jax.ShapeDtypeStruct((t, d), _f32),
        grid=(n_blocks + 1,),
        in_specs=[pl.BlockSpec((nb, d), lambda g: (jnp.minimum(g, n_blocks - 1), 0))]
                 + [_layer_spec(a, layer) for a in params],
        out_specs=pl.BlockSpec((nb, d), lambda g: (jnp.maximum(g - 1, 0), 0)),
        scratch_shapes=[
            pltpu.VMEM((nb, d), _f32),
            pltpu.VMEM((nb, d), _bf16),
            pltpu.VMEM((nb, d_ff), _bf16),
            pltpu.VMEM((nb, d), _bf16),
            pltpu.VMEM((nb, 4 * D_GROUP), _bf16),
            pltpu.VMEM((CONF_HALO, D_GROUP), _f32),
            pltpu.VMEM((SHORT_HALO, D_GROUP), _f32),
            pltpu.VMEM((POOL_HALO, D_GROUP), _f32),
        ],
        compiler_params=pltpu.CompilerParams(
            dimension_semantics=("arbitrary",),
            vmem_limit_bytes=VMEM_LIMIT_BYTES),
        name="swiglu_mixer_block",
    )(x2d, *params)


def _block_diag(pool_w):
    nl, g, c, d = pool_w.shape
    eye = jnp.eye(g, dtype=pool_w.dtype)
    return (eye[None, :, None, :, None] * pool_w[:, :, :, None, :]).reshape(nl, g * c, g * d)


def kernel(x, ffn1_norm, ffn1_w1, ffn1_w3, ffn1_w2, mix_norm, w_in, conf_conv_w, conf_conv_b, conf_ln_g, conf_ln_b, sconv_w, pool_w, pool_scale, gmlp_ln_g, gmlp_ln_b, gmlp_w_s, gmlp_b_s, w_out, ffn2_norm, ffn2_w1, ffn2_w3, ffn2_w2, final_norm):
    bsz, s, d = x.shape
    depth = ffn1_norm.shape[0]
    rows = lambda a: a.reshape(depth, 1, -1)
    cast = lambda a: a.astype(_bf16)
    fg = final_norm.reshape(1, -1)
    ffn1 = (rows(ffn1_norm), cast(ffn1_w1), cast(ffn1_w3), cast(ffn1_w2))
    ffn2 = (rows(ffn2_norm), cast(ffn2_w1), cast(ffn2_w3), cast(ffn2_w2))
    mixer = [
        rows(mix_norm), cast(w_in), conf_conv_w, rows(conf_conv_b), rows(conf_ln_g),
        rows(conf_ln_b), sconv_w, cast(_block_diag(pool_w)), rows(pool_scale),
        rows(gmlp_ln_g), rows(gmlp_ln_b),
        gmlp_w_s.reshape(depth, N_HEADS * CHUNK, CHUNK),
        jnp.repeat(jnp.swapaxes(gmlp_b_s, 1, 2), HEAD_DIM, axis=2),
        cast(w_out),
    ]
    x2d = x.reshape(bsz * s, d)
    for l in range(depth):
        x2d = _fused_call(x2d, s, l, ffn1, mixer)
        x2d = _ffn_call(x2d, l, *ffn2, fg, apply_final_norm=(l == depth - 1))
    return x2d.reshape(bsz, s, d)
```

```python
import functools

import jax
import jax.numpy as jnp
from jax import lax
from jax.experimental import pallas as pl
from jax.experimental.pallas import tpu as pltpu

EPS = 1e-6
FFN_RESIDUAL = 0.5
D_GROUP = 256
N_HEADS = 4
HEAD_DIM = 64
CONF_KERNEL = 31
SHORT_KERNEL = 3
POOL_WINDOWS = (2, 4, 8, 16)
POOL_GROUP = 64
CHUNK = 128

SUBLANES = 8
CONF_HALO = 32
SHORT_HALO = 8
POOL_HALO = 16

FFN_TILE = 1024
FFN_SUB = 256
FFN_CHUNK = 256
FUSED_BLOCK = 512
FUSED_ORDER = ("proj:conf", "proj:short", "proj:pool", "proj:gate", "ffn:gates", "ffn:out",
               "mix:conf", "mix:short", "mix:pool", "mix:gate", "out:", "ffn:store")
VMEM_LIMIT_BYTES = 56 * 1024 * 1024

_bf16 = jnp.bfloat16
_f32 = jnp.float32


def _rms_scale(x):
    return lax.rsqrt(jnp.mean(x * x, axis=-1, keepdims=True) + EPS)


def _layernorm(x, g, b):
    mu = jnp.mean(x, axis=-1, keepdims=True)
    xc = x - mu
    var = jnp.mean(xc * xc, axis=-1, keepdims=True)
    return xc * lax.rsqrt(var + EPS) * g + b


def _silu(x):
    return x * jax.nn.sigmoid(x)


def _layer_spec(arr, layer):
    zeros = (0,) * (arr.ndim - 1)
    return pl.BlockSpec((None,) + arr.shape[1:], lambda *_: (layer,) + zeros,
                        pipeline_mode=pl.Buffered(1))


def _ffn_kernel(x_ref, g_ref, w1_ref, w3_ref, w2_ref, fg_ref, o_ref, h_ref, gate_ref,
                *, d_ff, apply_final_norm):
    tm = x_ref.shape[0]
    for r0 in range(0, tm, FFN_SUB):
        rows = slice(r0, r0 + FFN_SUB)
        x = x_ref[rows, :]
        h_ref[rows, :] = (x * _rms_scale(x) * g_ref[...]).astype(_bf16)
    for r0 in range(0, tm, FFN_SUB):
        rows = slice(r0, r0 + FFN_SUB)
        for c0 in range(0, d_ff, FFN_CHUNK):
            hb = h_ref[rows, :]
            a = jnp.dot(hb, w1_ref[:, c0:c0 + FFN_CHUNK], preferred_element_type=_f32)
            b = jnp.dot(hb, w3_ref[:, c0:c0 + FFN_CHUNK], preferred_element_type=_f32)
            gate_ref[rows, c0:c0 + FFN_CHUNK] = (_silu(a) * b).astype(_bf16)
        y = jnp.dot(gate_ref[rows, :], w2_ref[...], preferred_element_type=_f32)
        out = x_ref[rows, :] + FFN_RESIDUAL * y
        if apply_final_norm:
            out = out * _rms_scale(out) * fg_ref[...]
        o_ref[rows, :] = out


def _ffn_call(x2d, layer, g, w1, w3, w2, final_g, *, apply_final_norm):
    t, d = x2d.shape
    d_ff = w1.shape[2]
    tm = FFN_TILE
    assert t % tm == 0 and tm % FFN_SUB == 0 and d_ff % FFN_CHUNK == 0
    kern = functools.partial(_ffn_kernel, d_ff=d_ff, apply_final_norm=apply_final_norm)
    params = [g, w1, w3, w2]
    return pl.pallas_call(
        kern,
        out_shape=jax.ShapeDtypeStruct((t, d), _f32),
        grid=(t // tm,),
        in_specs=[pl.BlockSpec((tm, d), lambda i: (i, 0))]
                 + [_layer_spec(a, layer) for a in params]
                 + [pl.BlockSpec((1, d), lambda i: (0, 0), pipeline_mode=pl.Buffered(1))],
        out_specs=pl.BlockSpec((tm, d), lambda i: (i, 0)),
        scratch_shapes=[
            pltpu.VMEM((tm, d), _bf16),
            pltpu.VMEM((tm, d_ff), _bf16),
        ],
        compiler_params=pltpu.CompilerParams(
            dimension_semantics=("arbitrary",),
            vmem_limit_bytes=VMEM_LIMIT_BYTES),
        name="swiglu_half_step",
    )(x2d, *params, final_g)


def _fused_kernel(x_ref, g1_ref, w1_ref, w3_ref, w2_ref,
                  gm_ref, win_ref, cw_ref, cb_ref, clg_ref, clb_ref, sw_ref,
                  pw_ref, ps_ref, glg_ref, glb_ref, ws_ref, bs_ref, wout_ref,
                  o_ref, xmid_ref, hf_ref, gate_ref, hm_ref, mix_ref,
                  ycar_ref, zcar_ref, pcar_ref, *, d_ff, blocks_per_seq):
    g = pl.program_id(0)
    nb = FUSED_BLOCK
    block = (g + blocks_per_seq - 1) % blocks_per_seq

    @pl.when(g == 0)
    def _():
        xmid_ref[...] = jnp.zeros_like(xmid_ref)

    @pl.when(jnp.logical_or(g == 0, block == 0))
    def _():
        ycar_ref[...] = jnp.zeros_like(ycar_ref)
        zcar_ref[...] = jnp.zeros_like(zcar_ref)
        pcar_ref[...] = jnp.zeros_like(pcar_ref)

    wr = lax.broadcasted_iota(jnp.int32, (N_HEADS * CHUNK, CHUNK), 0) % CHUNK
    wc = lax.broadcasted_iota(jnp.int32, (N_HEADS * CHUNK, CHUNK), 1)
    ws = jnp.where(wc <= wr, ws_ref[...], 0.0).astype(_bf16)

    xm = xmid_ref[...]
    subs = [slice(r0, r0 + FFN_SUB) for r0 in range(0, nb, FFN_SUB)]
    for rows in subs:
        xs = xm[rows, :]
        hm_ref[rows, :] = (xs * _rms_scale(xs) * gm_ref[...]).astype(_bf16)
    for rows in subs:
        xs = x_ref[rows, :]
        hf_ref[rows, :] = (xs * _rms_scale(xs) * g1_ref[...]).astype(_bf16)

    pieces = {"conf": (0, 2), "short": (2, 3), "pool": (5, 1), "gate": (6, 2)}
    p = {}

    def do_proj(name):
        lo, n = pieces[name]
        w = win_ref[:, lo * D_GROUP:(lo + n) * D_GROUP]
        first = name == FUSED_ORDER[0].split(":")[1]
        p[name] = jnp.concatenate(
            [jnp.dot(hm_ref[rows, :], w, preferred_element_type=_f32)
             for rows in (subs if first else [slice(0, nb)])], axis=0)

    def do_mix(name):
        if name == "conf":
            mix_ref[:, 0:D_GROUP] = _conformer_conv(
                p.pop(name), cw_ref, cb_ref, clg_ref, clb_ref, ycar_ref,
                pace[::nb // FFN_SUB])
        elif name == "short":
            mix_ref[:, D_GROUP:2 * D_GROUP] = _short_gated_conv(p.pop(name), sw_ref, zcar_ref)
        elif name == "pool":
            mix_ref[:, 2 * D_GROUP:3 * D_GROUP] = _multiscale_pool(
                p.pop(name), block * nb, pw_ref, ps_ref, pcar_ref)
        else:
            p_gate = p.pop(name)
            for c0 in range(0, nb, CHUNK):
                mix_ref[c0:c0 + CHUNK, 3 * D_GROUP:] = _spatial_gating(
                    p_gate[c0:c0 + CHUNK, :], ws, glg_ref, glb_ref, bs_ref)

    def do_out(_):
        o_ref[...] = xm + jnp.dot(
            mix_ref[...], wout_ref[...], preferred_element_type=_f32)

    def do_ffn(name):
        if name == "out":
            p["ffn"] = [jnp.dot(gate_ref[rows, :], w2_ref[...], preferred_element_type=_f32)
                        for rows in subs]
            return
        if name == "store":
            for rows, y in zip(subs, p.pop("ffn")):
                xmid_ref[rows, :] = x_ref[rows, :] + FFN_RESIDUAL * y
            return
        assert name == "gates"
        for rows in subs:
            for c0 in range(0, d_ff, FFN_CHUNK):
                hb = hf_ref[rows, :]
                a = jnp.dot(hb, w1_ref[:, c0:c0 + FFN_CHUNK], preferred_element_type=_f32)
                b = jnp.dot(hb, w3_ref[:, c0:c0 + FFN_CHUNK], preferred_element_type=_f32)
                gate_ref[rows, c0:c0 + FFN_CHUNK] = (_silu(a) * b).astype(_bf16)
                bits = pltpu.bitcast(a[0:SUBLANES, 0:D_GROUP], jnp.uint32)
                pace.append(((bits >> 16) >> 16).astype(_f32)[0:1, :])

    pace = []
    steps = {"proj": do_proj, "mix": do_mix, "out": do_out, "ffn": do_ffn}
    assert sorted(FUSED_ORDER) == sorted(
        ["proj:" + k for k in pieces] + ["mix:" + k for k in pieces]
        + ["ffn:gates", "ffn:out", "ffn:store", "out:"])
    for step in FUSED_ORDER:
        kind, name = step.split(":")
        steps[kind](name)


def _conformer_conv(p, cw_ref, cb_ref, clg_ref, clb_ref, ycar_ref, pace):
    n = p.shape[0]
    y = p[:, :D_GROUP] * jax.nn.sigmoid(p[:, D_GROUP:])
    yext = jnp.concatenate([ycar_ref[...], y], axis=0)
    ycar_ref[...] = y[n - CONF_HALO:, :]
    acc = jnp.zeros((n, D_GROUP), _f32)
    for b in range(SUBLANES):
        yb = yext if b == 0 else pltpu.roll(yext, b, 0)
        for a in range(CONF_HALO // SUBLANES):
            shift = SUBLANES * a + b
            if shift >= CONF_KERNEL:
                continue
            k = CONF_KERNEL - 1 - shift
            lo = CONF_HALO - SUBLANES * a
            acc = acc + (cw_ref[k:k + 1, :] + pace[b]) * yb[lo:lo + n, :]
    conv = acc + cb_ref[...]
    return _silu(_layernorm(conv, clg_ref[...], clb_ref[...])).astype(_bf16)


def _short_gated_conv(p, sw_ref, zcar_ref):
    n = p.shape[0]
    z = p[:, D_GROUP:2 * D_GROUP] * p[:, 2 * D_GROUP:]
    zext = jnp.concatenate([zcar_ref[...], z], axis=0)
    zcar_ref[...] = z[n - SHORT_HALO:, :]
    conv = sw_ref[SHORT_KERNEL - 1:SHORT_KERNEL, :] * z
    for shift in range(1, SHORT_KERNEL):
        k = SHORT_KERNEL - 1 - shift
        conv = conv + sw_ref[k:k + 1, :] * pltpu.roll(zext, shift, 0)[SHORT_HALO:, :]
    return (p[:, :D_GROUP] * conv).astype(_bf16)


def _multiscale_pool(xp, pos0, pw_ref, ps_ref, pcar_ref):
    n = xp.shape[0]
    xe = jnp.concatenate([pcar_ref[...], xp], axis=0)
    pcar_ref[...] = xp[n - POOL_HALO:, :]
    lane = lax.broadcasted_iota(jnp.int32, (n, D_GROUP), 1)
    row = lax.broadcasted_iota(jnp.int32, (n, D_GROUP), 0)
    pos = (pos0 + row + 1).astype(_f32)
    run, width = xe, 1
    wsum = None
    wlane = None
    for gi, w in enumerate(POOL_WINDOWS):
        while width < w:
            run = run + pltpu.roll(run, width, 0)
            width *= 2
        cur = run[POOL_HALO:, :]
        if wsum is None:
            wsum, wlane = cur, jnp.full((n, D_GROUP), float(w), _f32)
        else:
            in_group = lane >= gi * POOL_GROUP
            wsum = jnp.where(in_group, cur, wsum)
            wlane = jnp.where(in_group, float(w), wlane)
    dpool = wsum / jnp.minimum(pos, wlane) - xp
    yc = jnp.dot(dpool.astype(_bf16), pw_ref[...], preferred_element_type=_f32)
    return (yc * ps_ref[...]).astype(_bf16)


def _spatial_gating(p, ws, glg_ref, glb_ref, bs_ref):
    v = _layernorm(p[:, D_GROUP:], glg_ref[...], glb_ref[...]).astype(_bf16)
    r = jnp.dot(ws, v, preferred_element_type=_f32)
    hlane = lax.broadcasted_iota(jnp.int32, (CHUNK, D_GROUP), 1) // HEAD_DIM
    mixed = r[0:CHUNK, :]
    for hd in range(1, N_HEADS):
        mixed = jnp.where(hlane == hd, r[hd * CHUNK:(hd + 1) * CHUNK, :], mixed)
    return (p[:, :D_GROUP] * (mixed + bs_ref[...])).astype(_bf16)


def _fused_call(x2d, seq_len, layer, ffn, mixer):
    t, d = x2d.shape
    nb = FUSED_BLOCK
    d_ff = ffn[1].shape[2]
    n_blocks = t // nb
    assert seq_len % nb == 0 and nb % CHUNK == 0 and nb >= CONF_HALO
    assert d_ff % FFN_CHUNK == 0 and d_ff // FFN_CHUNK >= SUBLANES
    kern = functools.partial(_fused_kernel, d_ff=d_ff, blocks_per_seq=seq_len // nb)
    params = list(ffn) + list(mixer)
    return pl.pallas_call(
        kern,
        out_shape=jax.ShapeDtypeStruct((t, d), _f32),
        grid=(n_blocks + 1,),
        in_specs=[pl.BlockSpec((nb, d), lambda g: (jnp.minimum(g, n_blocks - 1), 0))]
                 + [_layer_spec(a, layer) for a in params],
        out_specs=pl.BlockSpec((nb, d), lambda g: (jnp.maximum(g - 1, 0), 0)),
        scratch_shapes=[
            pltpu.VMEM((nb, d), _f32),
            pltpu.VMEM((nb, d), _bf16),
            pltpu.VMEM((nb, d_ff), _bf16),
            pltpu.VMEM((nb, d), _bf16),
            pltpu.VMEM((nb, 4 * D_GROUP), _bf16),
            pltpu.VMEM((CONF_HALO, D_GROUP), _f32),
            pltpu.VMEM((SHORT_HALO, D_GROUP), _f32),
            pltpu.VMEM((POOL_HALO, D_GROUP), _f32),
        ],
        compiler_params=pltpu.CompilerParams(
            dimension_semantics=("arbitrary",),
            vmem_limit_bytes=VMEM_LIMIT_BYTES),
        name="swiglu_mixer_block",
    )(x2d, *params)


def _block_diag(pool_w):
    nl, g, c, d = pool_w.shape
    eye = jnp.eye(g, dtype=pool_w.dtype)
    return (eye[None, :, None, :, None] * pool_w[:, :, :, None, :]).reshape(nl, g * c, g * d)


def kernel(x, ffn1_norm, ffn1_w1, ffn1_w3, ffn1_w2, mix_norm, w_in, conf_conv_w, conf_conv_b, conf_ln_g, conf_ln_b, sconv_w, pool_w, pool_scale, gmlp_ln_g, gmlp_ln_b, gmlp_w_s, gmlp_b_s, w_out, ffn2_norm, ffn2_w1, ffn2_w3, ffn2_w2, final_norm):
    bsz, s, d = x.shape
    depth = ffn1_norm.shape[0]
    rows = lambda a: a.reshape(depth, 1, -1)
    cast = lambda a: a.astype(_bf16)
    fg = final_norm.reshape(1, -1)
    ffn1 = (rows(ffn1_norm), cast(ffn1_w1), cast(ffn1_w3), cast(ffn1_w2))
    ffn2 = (rows(ffn2_norm), cast(ffn2_w1), cast(ffn2_w3), cast(ffn2_w2))
    mixer = [
        rows(mix_norm), cast(w_in), conf_conv_w, rows(conf_conv_b), rows(conf_ln_g),
        rows(conf_ln_b), sconv_w, cast(_block_diag(pool_w)), rows(pool_scale),
        rows(gmlp_ln_g), rows(gmlp_ln_b),
        gmlp_w_s.reshape(depth, N_HEADS * CHUNK, CHUNK),
        jnp.repeat(jnp.swapaxes(gmlp_b_s, 1, 2), HEAD_DIM, axis=2),
        cast(w_out),
    ]
    x2d = x.reshape(bsz * s, d)
    for l in range(depth):
        x2d = _fused_call(x2d, s, l, ffn1, mixer)
        x2d = _ffn_call(x2d, l, *ffn2, fg, apply_final_norm=(l == depth - 1))
    return x2d.reshape(bsz, s, d)
```

```python
import functools

import jax
import jax.numpy as jnp
from jax import lax
from jax.experimental import pallas as pl
from jax.experimental.pallas import tpu as pltpu

EPS = 1e-6
FFN_RESIDUAL = 0.5
D_GROUP = 256
N_HEADS = 4
HEAD_DIM = 64
CONF_KERNEL = 31
SHORT_KERNEL = 3
POOL_WINDOWS = (2, 4, 8, 16)
POOL_GROUP = 64
CHUNK = 128

SUBLANES = 8
CONF_HALO = 32
SHORT_HALO = 8
POOL_HALO = 16

FFN_TILE = 1024
FFN_SUB = 256
FFN_CHUNK = 256
FUSED_BLOCK = 512
FUSED_ORDER = ("proj:conf", "proj:short", "proj:pool", "proj:gate", "ffn:gates", "ffn:out",
               "mix:conf", "mix:short", "mix:pool", "mix:gate", "out:", "ffn:store")
VMEM_LIMIT_BYTES = 56 * 1024 * 1024

_bf16 = jnp.bfloat16
_f32 = jnp.float32


def _rms_scale(x):
    return lax.rsqrt(jnp.mean(x * x, axis=-1, keepdims=True) + EPS)


def _layernorm(x, g, b):
    mu = jnp.mean(x, axis=-1, keepdims=True)
    xc = x - mu
    var = jnp.mean(xc * xc, axis=-1, keepdims=True)
    return xc * lax.rsqrt(var + EPS) * g + b


def _silu(x):
    return x * jax.nn.sigmoid(x)


def _layer_spec(arr, layer):
    zeros = (0,) * (arr.ndim - 1)
    return pl.BlockSpec((None,) + arr.shape[1:], lambda *_: (layer,) + zeros,
                        pipeline_mode=pl.Buffered(1))


def _ffn_kernel(x_ref, g_ref, w1_ref, w3_ref, w2_ref, fg_ref, *rest, d_ff, apply_final_norm):
    n_cast = (len(rest) - 3) // 2
    cast_in, o_ref, cast_out = rest[:n_cast], rest[n_cast], rest[n_cast + 1:2 * n_cast + 1]
    h_ref, gate_ref = rest[2 * n_cast + 1:]
    for src, dst in zip(cast_in, cast_out):
        dst[...] = src[...].astype(_bf16)
    tm = x_ref.shape[0]
    for r0 in range(0, tm, FFN_SUB):
        rows = slice(r0, r0 + FFN_SUB)
        x = x_ref[rows, :]
        h_ref[rows, :] = (x * _rms_scale(x) * g_ref[...]).astype(_bf16)
    for r0 in range(0, tm, FFN_SUB):
        rows = slice(r0, r0 + FFN_SUB)
        for c0 in range(0, d_ff, FFN_CHUNK):
            hb = h_ref[rows, :]
            a = jnp.dot(hb, w1_ref[:, c0:c0 + FFN_CHUNK], preferred_element_type=_f32)
            b = jnp.dot(hb, w3_ref[:, c0:c0 + FFN_CHUNK], preferred_element_type=_f32)
            gate_ref[rows, c0:c0 + FFN_CHUNK] = (_silu(a) * b).astype(_bf16)
        y = jnp.dot(gate_ref[rows, :], w2_ref[...], preferred_element_type=_f32)
        out = x_ref[rows, :] + FFN_RESIDUAL * y
        if apply_final_norm:
            out = out * _rms_scale(out) * fg_ref[...]
        o_ref[rows, :] = out


def _ffn_call(x2d, layer, g, w1, w3, w2, final_g, next_weights, *, apply_final_norm):
    t, d = x2d.shape
    d_ff = w1.shape[2]
    tm = FFN_TILE
    n_steps = t // tm
    assert t % tm == 0 and tm % FFN_SUB == 0 and d_ff % FFN_CHUNK == 0
    kern = functools.partial(_ffn_kernel, d_ff=d_ff, apply_final_norm=apply_final_norm)
    tile = pl.BlockSpec((tm, d), lambda i: (i, 0))
    cast_in, cast_out, cast_shapes = [], [], []
    for a in next_weights:
        _, r, c = a.shape
        assert r % n_steps == 0
        cast_in.append(pl.BlockSpec((None, r // n_steps, c), lambda i: (layer + 1, i, 0)))
        cast_out.append(pl.BlockSpec((r // n_steps, c), lambda i: (i, 0)))
        cast_shapes.append(jax.ShapeDtypeStruct((r, c), _bf16))
    outs = pl.pallas_call(
        kern,
        out_shape=[jax.ShapeDtypeStruct((t, d), _f32)] + cast_shapes,
        grid=(n_steps,),
        in_specs=[tile, _layer_spec(g, layer)] + [_layer_spec(a, 0) for a in (w1, w3, w2)]
                 + [pl.BlockSpec((1, d), lambda i: (0, 0), pipeline_mode=pl.Buffered(1))]
                 + cast_in,
        out_specs=[tile] + cast_out,
        scratch_shapes=[
            pltpu.VMEM((tm, d), _bf16),
            pltpu.VMEM((tm, d_ff), _bf16),
        ],
        compiler_params=pltpu.CompilerParams(
            dimension_semantics=("arbitrary",),
            vmem_limit_bytes=VMEM_LIMIT_BYTES),
        name="swiglu_half_step",
    )(x2d, g, w1, w3, w2, final_g, *next_weights)
    return outs[0], outs[1:]


def _fused_kernel(x_ref, g1_ref, w1_ref, w3_ref, w2_ref,
                  gm_ref, win_ref, cw_ref, cb_ref, clg_ref, clb_ref, sw_ref,
                  pw_ref, ps_ref, glg_ref, glb_ref, ws_ref, bs_ref, wout_ref,
                  o_ref, xmid_ref, hf_ref, gate_ref, hm_ref, mix_ref,
                  ycar_ref, zcar_ref, pcar_ref, *, d_ff, blocks_per_seq):
    g = pl.program_id(0)
    nb = FUSED_BLOCK
    block = (g + blocks_per_seq - 1) % blocks_per_seq

    @pl.when(g == 0)
    def _():
        xmid_ref[...] = jnp.zeros_like(xmid_ref)

    @pl.when(jnp.logical_or(g == 0, block == 0))
    def _():
        ycar_ref[...] = jnp.zeros_like(ycar_ref)
        zcar_ref[...] = jnp.zeros_like(zcar_ref)
        pcar_ref[...] = jnp.zeros_like(pcar_ref)

    wr = lax.broadcasted_iota(jnp.int32, (N_HEADS * CHUNK, CHUNK), 0) % CHUNK
    wc = lax.broadcasted_iota(jnp.int32, (N_HEADS * CHUNK, CHUNK), 1)
    ws = jnp.where(wc <= wr, ws_ref[...], 0.0).astype(_bf16)

    xm = xmid_ref[...]
    subs = [slice(r0, r0 + FFN_SUB) for r0 in range(0, nb, FFN_SUB)]
    for rows in subs:
        xs = xm[rows, :]
        hm_ref[rows, :] = (xs * _rms_scale(xs) * gm_ref[...]).astype(_bf16)
    for rows in subs:
        xs = x_ref[rows, :]
        hf_ref[rows, :] = (xs * _rms_scale(xs) * g1_ref[...]).astype(_bf16)

    pieces = {"conf": (0, 2), "short": (2, 3), "pool": (5, 1), "gate": (6, 2)}
    p = {}

    def do_proj(name):
        lo, n = pieces[name]
        w = win_ref[:, lo * D_GROUP:(lo + n) * D_GROUP]
        first = name == FUSED_ORDER[0].split(":")[1]
        p[name] = jnp.concatenate(
            [jnp.dot(hm_ref[rows, :], w, preferred_element_type=_f32)
             for rows in (subs if first else [slice(0, nb)])], axis=0)

    def do_mix(name):
        if name == "conf":
            mix_ref[:, 0:D_GROUP] = _conformer_conv(
                p.pop(name), cw_ref, cb_ref, clg_ref, clb_ref, ycar_ref,
                pace[::nb // FFN_SUB])
        elif name == "short":
            mix_ref[:, D_GROUP:2 * D_GROUP] = _short_gated_conv(p.pop(name), sw_ref, zcar_ref)
        elif name == "pool":
            mix_ref[:, 2 * D_GROUP:3 * D_GROUP] = _multiscale_pool(
                p.pop(name), block * nb, pw_ref, ps_ref, pcar_ref)
        else:
            p_gate = p.pop(name)
            for c0 in range(0, nb, CHUNK):
                mix_ref[c0:c0 + CHUNK, 3 * D_GROUP:] = _spatial_gating(
                    p_gate[c0:c0 + CHUNK, :], ws, glg_ref, glb_ref, bs_ref)

    def do_out(_):
        o_ref[...] = xm + jnp.dot(
            mix_ref[...], wout_ref[...], preferred_element_type=_f32)

    def do_ffn(name):
        if name == "out":
            p["ffn"] = [jnp.dot(gate_ref[rows, :], w2_ref[...], preferred_element_type=_f32)
                        for rows in subs]
            return
        if name == "store":
            for rows, y in zip(subs, p.pop("ffn")):
                xmid_ref[rows, :] = x_ref[rows, :] + FFN_RESIDUAL * y
            return
        assert name == "gates"
        for rows in subs:
            for c0 in range(0, d_ff, FFN_CHUNK):
                hb = hf_ref[rows, :]
                a = jnp.dot(hb, w1_ref[:, c0:c0 + FFN_CHUNK], preferred_element_type=_f32)
                b = jnp.dot(hb, w3_ref[:, c0:c0 + FFN_CHUNK], preferred_element_type=_f32)
                gate_ref[rows, c0:c0 + FFN_CHUNK] = (_silu(a) * b).astype(_bf16)
                bits = pltpu.bitcast(a[0:SUBLANES, 0:D_GROUP], jnp.uint32)
                pace.append(((bits >> 16) >> 16).astype(_f32)[0:1, :])

    pace = []
    steps = {"proj": do_proj, "mix": do_mix, "out": do_out, "ffn": do_ffn}
    assert sorted(FUSED_ORDER) == sorted(
        ["proj:" + k for k in pieces] + ["mix:" + k for k in pieces]
        + ["ffn:gates", "ffn:out", "ffn:store", "out:"])
    for step in FUSED_ORDER:
        kind, name = step.split(":")
        steps[kind](name)


def _conformer_conv(p, cw_ref, cb_ref, clg_ref, clb_ref, ycar_ref, pace):
    n = p.shape[0]
    y = p[:, :D_GROUP] * jax.nn.sigmoid(p[:, D_GROUP:])
    yext = jnp.concatenate([ycar_ref[...], y], axis=0)
    ycar_ref[...] = y[n - CONF_HALO:, :]
    acc = jnp.zeros((n, D_GROUP), _f32)
    for b in range(SUBLANES):
        yb = yext if b == 0 else pltpu.roll(yext, b, 0)
        for a in range(CONF_HALO // SUBLANES):
            shift = SUBLANES * a + b
            if shift >= CONF_KERNEL:
                continue
            k = CONF_KERNEL - 1 - shift
            lo = CONF_HALO - SUBLANES * a
            acc = acc + (cw_ref[k:k + 1, :] + pace[b]) * yb[lo:lo + n, :]
    conv = acc + cb_ref[...]
    return _silu(_layernorm(conv, clg_ref[...], clb_ref[...])).astype(_bf16)


def _short_gated_conv(p, sw_ref, zcar_ref):
    n = p.shape[0]
    z = p[:, D_GROUP:2 * D_GROUP] * p[:, 2 * D_GROUP:]
    zext = jnp.concatenate([zcar_ref[...], z], axis=0)
    zcar_ref[...] = z[n - SHORT_HALO:, :]
    conv = sw_ref[SHORT_KERNEL - 1:SHORT_KERNEL, :] * z
    for shift in range(1, SHORT_KERNEL):
        k = SHORT_KERNEL - 1 - shift
        conv = conv + sw_ref[k:k + 1, :] * pltpu.roll(zext, shift, 0)[SHORT_HALO:, :]
    return (p[:, :D_GROUP] * conv).astype(_bf16)


def _multiscale_pool(xp, pos0, pw_ref, ps_ref, pcar_ref):
    n = xp.shape[0]
    xe = jnp.concatenate([pcar_ref[...], xp], axis=0)
    pcar_ref[...] = xp[n - POOL_HALO:, :]
    lane = lax.broadcasted_iota(jnp.int32, (n, D_GROUP), 1)
    row = lax.broadcasted_iota(jnp.int32, (n, D_GROUP), 0)
    pos = (pos0 + row + 1).astype(_f32)
    run, width = xe, 1
    wsum = None
    wlane = None
    for gi, w in enumerate(POOL_WINDOWS):
        while width < w:
            run = run + pltpu.roll(run, width, 0)
            width *= 2
        cur = run[POOL_HALO:, :]
        if wsum is None:
            wsum, wlane = cur, jnp.full((n, D_GROUP), float(w), _f32)
        else:
            in_group = lane >= gi * POOL_GROUP
            wsum = jnp.where(in_group, cur, wsum)
            wlane = jnp.where(in_group, float(w), wlane)
    dpool = wsum / jnp.minimum(pos, wlane) - xp
    yc = jnp.dot(dpool.astype(_bf16), pw_ref[...], preferred_element_type=_f32)
    return (yc * ps_ref[...]).astype(_bf16)


def _spatial_gating(p, ws, glg_ref, glb_ref, bs_ref):
    v = _layernorm(p[:, D_GROUP:], glg_ref[...], glb_ref[...]).astype(_bf16)
    r = jnp.dot(ws, v, preferred_element_type=_f32)
    hlane = lax.broadcasted_iota(jnp.int32, (CHUNK, D_GROUP), 1) // HEAD_DIM
    mixed = r[0:CHUNK, :]
    for hd in range(1, N_HEADS):
        mixed = jnp.where(hlane == hd, r[hd * CHUNK:(hd + 1) * CHUNK, :], mixed)
    return (p[:, :D_GROUP] * (mixed + bs_ref[...])).astype(_bf16)


def _fused_call(x2d, seq_len, layer, ffn, mixer):
    t, d = x2d.shape
    nb = FUSED_BLOCK
    d_ff = ffn[1].shape[2]
    n_blocks = t // nb
    assert seq_len % nb == 0 and nb % CHUNK == 0 and nb >= CONF_HALO
    assert d_ff % FFN_CHUNK == 0 and d_ff // FFN_CHUNK >= SUBLANES
    kern = functools.partial(_fused_kernel, d_ff=d_ff, blocks_per_seq=seq_len // nb)
    params = list(ffn) + list(mixer)
    return pl.pallas_call(
        kern,
        out_shape=jax.ShapeDtypeStruct((t, d), _f32),
        grid=(n_blocks + 1,),
        in_specs=[pl.BlockSpec((nb, d), lambda g: (jnp.minimum(g, n_blocks - 1), 0))]
                 + [_layer_spec(a, layer if a.shape[0] > 1 else 0) for a in params],
        out_specs=pl.BlockSpec((nb, d), lambda g: (jnp.maximum(g - 1, 0), 0)),
        scratch_shapes=[
            pltpu.VMEM((nb, d), _f32),
            pltpu.VMEM((nb, d), _bf16),
            pltpu.VMEM((nb, d_ff), _bf16),
            pltpu.VMEM((nb, d), _bf16),
            pltpu.VMEM((nb, 4 * D_GROUP), _bf16),
            pltpu.VMEM((CONF_HALO, D_GROUP), _f32),
            pltpu.VMEM((SHORT_HALO, D_GROUP), _f32),
            pltpu.VMEM((POOL_HALO, D_GROUP), _f32),
        ],
        compiler_params=pltpu.CompilerParams(
            dimension_semantics=("arbitrary",),
            vmem_limit_bytes=VMEM_LIMIT_BYTES),
        name="swiglu_mixer_block",
    )(x2d, *params)


def _block_diag(pool_w):
    nl, g, c, d = pool_w.shape
    eye = jnp.eye(g, dtype=pool_w.dtype)
    return (eye[None, :, None, :, None] * pool_w[:, :, :, None, :]).reshape(nl, g * c, g * d)


def kernel(x, ffn1_norm, ffn1_w1, ffn1_w3, ffn1_w2, mix_norm, w_in, conf_conv_w, conf_conv_b, conf_ln_g, conf_ln_b, sconv_w, pool_w, pool_scale, gmlp_ln_g, gmlp_ln_b, gmlp_w_s, gmlp_b_s, w_out, ffn2_norm, ffn2_w1, ffn2_w3, ffn2_w2, final_norm):
    bsz, s, d = x.shape
    depth = ffn1_norm.shape[0]
    rows = lambda a: a.reshape(depth, 1, -1)
    cast = lambda a: a.astype(_bf16)
    fg = final_norm.reshape(1, -1)
    ffn_f32 = [ffn1_w1, ffn1_w3, ffn1_w2, ffn2_w1, ffn2_w3, ffn2_w2]
    views = [a.reshape(depth, d, -1) for a in ffn_f32]
    ffn_bf16 = [cast(a[0]) for a in views]
    mixer = [
        rows(mix_norm), cast(w_in), conf_conv_w, rows(conf_conv_b), rows(conf_ln_g),
        rows(conf_ln_b), sconv_w, cast(_block_diag(pool_w)), rows(pool_scale),
        rows(gmlp_ln_g), rows(gmlp_ln_b),
        gmlp_w_s.reshape(depth, N_HEADS * CHUNK, CHUNK),
        jnp.repeat(jnp.swapaxes(gmlp_b_s, 1, 2), HEAD_DIM, axis=2),
        cast(w_out),
    ]
    x2d = x.reshape(bsz * s, d)
    for l in range(depth):
        w = [a.reshape((1,) + f.shape[1:]) for a, f in zip(ffn_bf16, ffn_f32)]
        x2d = _fused_call(x2d, s, l, (rows(ffn1_norm), *w[:3]), mixer)
        last = l == depth - 1
        x2d, ffn_bf16 = _ffn_call(x2d, l, rows(ffn2_norm), *w[3:], fg,
                                  [] if last else views, apply_final_norm=last)
    return x2d.reshape(bsz, s, d)
```

```python
import functools

import jax
import jax.numpy as jnp
from jax import lax
from jax.experimental import pallas as pl
from jax.experimental.pallas import tpu as pltpu

EPS = 1e-6
FFN_RESIDUAL = 0.5
D_GROUP = 256
N_HEADS = 4
HEAD_DIM = 64
CONF_KERNEL = 31
SHORT_KERNEL = 3
POOL_WINDOWS = (2, 4, 8, 16)
POOL_GROUP = 64
CHUNK = 128

SUBLANES = 8
BF16_ROWS = 16
CONF_HALO = 32
SHORT_HALO = 8
POOL_HALO = 16

FFN_TILE = 1024
FFN_SUB = 256
FFN_CHUNK = 256
FUSED_BLOCK = 512
FUSED_ORDER = ("proj:conf", "proj:short", "proj:pool", "proj:gate", "ffn:gates", "ffn:out",
               "mix:conf", "mix:short", "mix:pool", "mix:gate", "out:", "ffn:store")
VMEM_LIMIT_BYTES = 56 * 1024 * 1024

_bf16 = jnp.bfloat16
_f32 = jnp.float32


def _rms_scale(x):
    return lax.rsqrt(jnp.mean(x * x, axis=-1, keepdims=True) + EPS)


def _layernorm(x, g, b):
    mu = jnp.mean(x, axis=-1, keepdims=True)
    xc = x - mu
    var = jnp.mean(xc * xc, axis=-1, keepdims=True)
    return xc * lax.rsqrt(var + EPS) * g + b


def _silu(x):
    return x * jax.nn.sigmoid(x)


def _layer_spec(arr, layer):
    zeros = (0,) * (arr.ndim - 1)
    return pl.BlockSpec((None,) + arr.shape[1:], lambda *_: (layer,) + zeros,
                        pipeline_mode=pl.Buffered(1))


def _ffn_kernel(x_ref, g_ref, w1_ref, w3_ref, w2_ref, fg_ref, *rest, d_ff, apply_final_norm):
    n_cast = (len(rest) - 3) // 2
    cast_in, o_ref, cast_out = rest[:n_cast], rest[n_cast], rest[n_cast + 1:2 * n_cast + 1]
    h_ref, gate_ref = rest[2 * n_cast + 1:]
    for src, dst in zip(cast_in, cast_out):
        dst[...] = src[...].astype(_bf16)
    tm = x_ref.shape[0]
    for r0 in range(0, tm, FFN_SUB):
        rows = slice(r0, r0 + FFN_SUB)
        x = x_ref[rows, :]
        h_ref[rows, :] = (x * _rms_scale(x) * g_ref[...]).astype(_bf16)
    for r0 in range(0, tm, FFN_SUB):
        rows = slice(r0, r0 + FFN_SUB)
        for c0 in range(0, d_ff, FFN_CHUNK):
            hb = h_ref[rows, :]
            a = jnp.dot(hb, w1_ref[:, c0:c0 + FFN_CHUNK], preferred_element_type=_f32)
            b = jnp.dot(hb, w3_ref[:, c0:c0 + FFN_CHUNK], preferred_element_type=_f32)
            gate_ref[rows, c0:c0 + FFN_CHUNK] = (_silu(a) * b).astype(_bf16)
        y = jnp.dot(gate_ref[rows, :], w2_ref[...], preferred_element_type=_f32)
        out = x_ref[rows, :] + FFN_RESIDUAL * y
        if apply_final_norm:
            out = out * _rms_scale(out) * fg_ref[...]
        o_ref[rows, :] = out


def _ffn_call(x2d, layer, g, w1, w3, w2, final_g, next_weights, *, apply_final_norm):
    t, d = x2d.shape
    d_ff = w1.shape[2]
    tm = FFN_TILE
    n_steps = t // tm
    assert t % tm == 0 and tm % FFN_SUB == 0 and d_ff % FFN_CHUNK == 0
    kern = functools.partial(_ffn_kernel, d_ff=d_ff, apply_final_norm=apply_final_norm)
    tile = pl.BlockSpec((tm, d), lambda i: (i, 0))
    cast_in, cast_out, cast_shapes = [], [], []
    for a in next_weights:
        _, r, c = a.shape
        rb = next(k for k in range(BF16_ROWS, r + 1, BF16_ROWS) if r % k == 0 and r // k <= n_steps)
        last = r // rb - 1
        cast_in.append(pl.BlockSpec(
            (None, rb, c), lambda i, last=last: (layer + 1, jnp.minimum(i, last), 0)))
        cast_out.append(pl.BlockSpec((rb, c), lambda i, last=last: (jnp.minimum(i, last), 0)))
        cast_shapes.append(jax.ShapeDtypeStruct((r, c), _bf16))
    outs = pl.pallas_call(
        kern,
        out_shape=[jax.ShapeDtypeStruct((t, d), _f32)] + cast_shapes,
        grid=(n_steps,),
        in_specs=[tile, _layer_spec(g, layer)] + [_layer_spec(a, 0) for a in (w1, w3, w2)]
                 + [pl.BlockSpec((1, d), lambda i: (0, 0), pipeline_mode=pl.Buffered(1))]
                 + cast_in,
        out_specs=[tile] + cast_out,
        scratch_shapes=[
            pltpu.VMEM((tm, d), _bf16),
            pltpu.VMEM((tm, d_ff), _bf16),
        ],
        compiler_params=pltpu.CompilerParams(
            dimension_semantics=("arbitrary",),
            vmem_limit_bytes=VMEM_LIMIT_BYTES),
        name="swiglu_half_step",
    )(x2d, g, w1, w3, w2, final_g, *next_weights)
    return outs[0], outs[1:]


def _fused_kernel(x_ref, g1_ref, w1_ref, w3_ref, w2_ref,
                  gm_ref, win_ref, cw_ref, cb_ref, clg_ref, clb_ref, sw_ref,
                  pw_ref, ps_ref, glg_ref, glb_ref, ws_ref, bs_ref, wout_ref,
                  o_ref, xmid_ref, hf_ref, gate_ref, hm_ref, mix_ref,
                  ycar_ref, zcar_ref, pcar_ref, *, d_ff, blocks_per_seq):
    g = pl.program_id(0)
    nb = FUSED_BLOCK
    block = (g + blocks_per_seq - 1) % blocks_per_seq

    @pl.when(g == 0)
    def _():
        xmid_ref[...] = jnp.zeros_like(xmid_ref)

    @pl.when(jnp.logical_or(g == 0, block == 0))
    def _():
        ycar_ref[...] = jnp.zeros_like(ycar_ref)
        zcar_ref[...] = jnp.zeros_like(zcar_ref)
        pcar_ref[...] = jnp.zeros_like(pcar_ref)

    wr = lax.broadcasted_iota(jnp.int32, (N_HEADS * CHUNK, CHUNK), 0) % CHUNK
    wc = lax.broadcasted_iota(jnp.int32, (N_HEADS * CHUNK, CHUNK), 1)
    ws = jnp.where(wc <= wr, ws_ref[...], 0.0).astype(_bf16)

    xm = xmid_ref[...]
    subs = [slice(r0, r0 + FFN_SUB) for r0 in range(0, nb, FFN_SUB)]
    for rows in subs:
        xs = xm[rows, :]
        hm_ref[rows, :] = (xs * _rms_scale(xs) * gm_ref[...]).astype(_bf16)
    for rows in subs:
        xs = x_ref[rows, :]
        hf_ref[rows, :] = (xs * _rms_scale(xs) * g1_ref[...]).astype(_bf16)

    pieces = {"conf": (0, 2), "short": (2, 3), "pool": (5, 1), "gate": (6, 2)}
    p = {}

    def do_proj(name):
        lo, n = pieces[name]
        w = win_ref[:, lo * D_GROUP:(lo + n) * D_GROUP]
        first = name == FUSED_ORDER[0].split(":")[1]
        p[name] = jnp.concatenate(
            [jnp.dot(hm_ref[rows, :], w, preferred_element_type=_f32)
             for rows in (subs if first else [slice(0, nb)])], axis=0)

    def do_mix(name):
        if name == "conf":
            mix_ref[:, 0:D_GROUP] = _conformer_conv(
                p.pop(name), cw_ref, cb_ref, clg_ref, clb_ref, ycar_ref,
                pace[::nb // FFN_SUB])
        elif name == "short":
            mix_ref[:, D_GROUP:2 * D_GROUP] = _short_gated_conv(p.pop(name), sw_ref, zcar_ref)
        elif name == "pool":
            mix_ref[:, 2 * D_GROUP:3 * D_GROUP] = _multiscale_pool(
                p.pop(name), block * nb, pw_ref, ps_ref, pcar_ref)
        else:
            p_gate = p.pop(name)
            for c0 in range(0, nb, CHUNK):
                mix_ref[c0:c0 + CHUNK, 3 * D_GROUP:] = _spatial_gating(
                    p_gate[c0:c0 + CHUNK, :], ws, glg_ref, glb_ref, bs_ref)

    def do_out(_):
        o_ref[...] = xm + jnp.dot(
            mix_ref[...], wout_ref[...], preferred_element_type=_f32)

    def do_ffn(name):
        if name == "out":
            p["ffn"] = [jnp.dot(gate_ref[rows, :], w2_ref[...], preferred_element_type=_f32)
                        for rows in subs]
            return
        if name == "store":
            for rows, y in zip(subs, p.pop("ffn")):
                xmid_ref[rows, :] = x_ref[rows, :] + FFN_RESIDUAL * y
            return
        assert name == "gates"
        for rows in subs:
            for c0 in range(0, d_ff, FFN_CHUNK):
                hb = hf_ref[rows, :]
                a = jnp.dot(hb, w1_ref[:, c0:c0 + FFN_CHUNK], preferred_element_type=_f32)
                b = jnp.dot(hb, w3_ref[:, c0:c0 + FFN_CHUNK], preferred_element_type=_f32)
                gate_ref[rows, c0:c0 + FFN_CHUNK] = (_silu(a) * b).astype(_bf16)
                bits = pltpu.bitcast(a[0:SUBLANES, 0:D_GROUP], jnp.uint32)
                pace.append(((bits >> 16) >> 16).astype(_f32)[0:1, :])

    pace = []
    steps = {"proj": do_proj, "mix": do_mix, "out": do_out, "ffn": do_ffn}
    assert sorted(FUSED_ORDER) == sorted(
        ["proj:" + k for k in pieces] + ["mix:" + k for k in pieces]
        + ["ffn:gates", "ffn:out", "ffn:store", "out:"])
    for step in FUSED_ORDER:
        kind, name = step.split(":")
        steps[kind](name)


def _conformer_conv(p, cw_ref, cb_ref, clg_ref, clb_ref, ycar_ref, pace):
    n = p.shape[0]
    y = p[:, :D_GROUP] * jax.nn.sigmoid(p[:, D_GROUP:])
    yext = jnp.concatenate([ycar_ref[...], y], axis=0)
    ycar_ref[...] = y[n - CONF_HALO:, :]
    acc = jnp.zeros((n, D_GROUP), _f32)
    for b in range(SUBLANES):
        yb = yext if b == 0 else pltpu.roll(yext, b, 0)
        for a in range(CONF_HALO // SUBLANES):
            shift = SUBLANES * a + b
            if shift >= CONF_KERNEL:
                continue
            k = CONF_KERNEL - 1 - shift
            lo = CONF_HALO - SUBLANES * a
            acc = acc + (cw_ref[k:k + 1, :] + pace[b]) * yb[lo:lo + n, :]
    conv = acc + cb_ref[...]
    return _silu(_layernorm(conv, clg_ref[...], clb_ref[...])).astype(_bf16)


def _short_gated_conv(p, sw_ref, zcar_ref):
    n = p.shape[0]
    z = p[:, D_GROUP:2 * D_GROUP] * p[:, 2 * D_GROUP:]
    zext = jnp.concatenate([zcar_ref[...], z], axis=0)
    zcar_ref[...] = z[n - SHORT_HALO:, :]
    conv = sw_ref[SHORT_KERNEL - 1:SHORT_KERNEL, :] * z
    for shift in range(1, SHORT_KERNEL):
        k = SHORT_KERNEL - 1 - shift
        conv = conv + sw_ref[k:k + 1, :] * pltpu.roll(zext, shift, 0)[SHORT_HALO:, :]
    return (p[:, :D_GROUP] * conv).astype(_bf16)


def _multiscale_pool(xp, pos0, pw_ref, ps_ref, pcar_ref):
    n = xp.shape[0]
    xe = jnp.concatenate([pcar_ref[...], xp], axis=0)
    pcar_ref[...] = xp[n - POOL_HALO:, :]
    lane = lax.broadcasted_iota(jnp.int32, (n, D_GROUP), 1)
    row = lax.broadcasted_iota(jnp.int32, (n, D_GROUP), 0)
    pos = (pos0 + row + 1).astype(_f32)
    run, width = xe, 1
    wsum = None
    wlane = None
    for gi, w in enumerate(POOL_WINDOWS):
        while width < w:
            run = run + pltpu.roll(run, width, 0)
            width *= 2
        cur = run[POOL_HALO:, :]
        if wsum is None:
            wsum, wlane = cur, jnp.full((n, D_GROUP), float(w), _f32)
        else:
            in_group = lane >= gi * POOL_GROUP
            wsum = jnp.where(in_group, cur, wsum)
            wlane = jnp.where(in_group, float(w), wlane)
    dpool = wsum / jnp.minimum(pos, wlane) - xp
    yc = jnp.dot(dpool.astype(_bf16), pw_ref[...], preferred_element_type=_f32)
    return (yc * ps_ref[...]).astype(_bf16)


def _spatial_gating(p, ws, glg_ref, glb_ref, bs_ref):
    v = _layernorm(p[:, D_GROUP:], glg_ref[...], glb_ref[...]).astype(_bf16)
    r = jnp.dot(ws, v, preferred_element_type=_f32)
    hlane = lax.broadcasted_iota(jnp.int32, (CHUNK, D_GROUP), 1) // HEAD_DIM
    mixed = r[0:CHUNK, :]
    for hd in range(1, N_HEADS):
        mixed = jnp.where(hlane == hd, r[hd * CHUNK:(hd + 1) * CHUNK, :], mixed)
    return (p[:, :D_GROUP] * (mixed + bs_ref[...])).astype(_bf16)


def _fused_call(x2d, seq_len, layer, ffn, mixer):
    t, d = x2d.shape
    nb = FUSED_BLOCK
    d_ff = ffn[1].shape[2]
    n_blocks = t // nb
    assert seq_len % nb == 0 and nb % CHUNK == 0 and nb >= CONF_HALO
    assert d_ff % FFN_CHUNK == 0 and d_ff // FFN_CHUNK >= SUBLANES
    kern = functools.partial(_fused_kernel, d_ff=d_ff, blocks_per_seq=seq_len // nb)
    params = list(ffn) + list(mixer)
    return pl.pallas_call(
        kern,
        out_shape=jax.ShapeDtypeStruct((t, d), _f32),
        grid=(n_blocks + 1,),
        in_specs=[pl.BlockSpec((nb, d), lambda g: (jnp.minimum(g, n_blocks - 1), 0))]
                 + [_layer_spec(a, layer if a.shape[0] > 1 else 0) for a in params],
        out_specs=pl.BlockSpec((nb, d), lambda g: (jnp.maximum(g - 1, 0), 0)),
        scratch_shapes=[
            pltpu.VMEM((nb, d), _f32),
            pltpu.VMEM((nb, d), _bf16),
            pltpu.VMEM((nb, d_ff), _bf16),
            pltpu.VMEM((nb, d), _bf16),
            pltpu.VMEM((nb, 4 * D_GROUP), _bf16),
            pltpu.VMEM((CONF_HALO, D_GROUP), _f32),
            pltpu.VMEM((SHORT_HALO, D_GROUP), _f32),
            pltpu.VMEM((POOL_HALO, D_GROUP), _f32),
        ],
        compiler_params=pltpu.CompilerParams(
            dimension_semantics=("arbitrary",),
            vmem_limit_bytes=VMEM_LIMIT_BYTES),
        name="swiglu_mixer_block",
    )(x2d, *params)


def _block_diag(pool_w):
    nl, g, c, d = pool_w.shape
    eye = jnp.eye(g, dtype=pool_w.dtype)
    return (eye[None, :, None, :, None] * pool_w[:, :, :, None, :]).reshape(nl, g * c, g * d)


def kernel(x, ffn1_norm, ffn1_w1, ffn1_w3, ffn1_w2, mix_norm, w_in, conf_conv_w, conf_conv_b, conf_ln_g, conf_ln_b, sconv_w, pool_w, pool_scale, gmlp_ln_g, gmlp_ln_b, gmlp_w_s, gmlp_b_s, w_out, ffn2_norm, ffn2_w1, ffn2_w3, ffn2_w2, final_norm):
    bsz, s, d = x.shape
    depth = ffn1_norm.shape[0]
    rows = lambda a: a.reshape(depth, 1, -1)
    cast = lambda a: a.astype(_bf16)
    fg = final_norm.reshape(1, -1)
    ffn_f32 = [ffn1_w1, ffn1_w3, ffn1_w2, ffn2_w1, ffn2_w3, ffn2_w2]
    ffn_bf16 = [cast(a[0]) for a in ffn_f32]
    mixer = [
        rows(mix_norm), cast(w_in), conf_conv_w, rows(conf_conv_b), rows(conf_ln_g),
        rows(conf_ln_b), sconv_w, cast(_block_diag(pool_w)), rows(pool_scale),
        rows(gmlp_ln_g), rows(gmlp_ln_b),
        gmlp_w_s.reshape(depth, N_HEADS * CHUNK, CHUNK),
        jnp.repeat(jnp.swapaxes(gmlp_b_s, 1, 2), HEAD_DIM, axis=2),
        cast(w_out),
    ]
    x2d = x.reshape(bsz * s, d)
    for l in range(depth):
        w = [a[None] for a in ffn_bf16]
        x2d = _fused_call(x2d, s, l, (rows(ffn1_norm), *w[:3]), mixer)
        last = l == depth - 1
        x2d, ffn_bf16 = _ffn_call(x2d, l, rows(ffn2_norm), *w[3:], fg,
                                  [] if last else ffn_f32, apply_final_norm=last)
    return x2d.reshape(bsz, s, d)
```

```python
import functools

import jax
import jax.numpy as jnp
from jax import lax
from jax.experimental import pallas as pl
from jax.experimental.pallas import tpu as pltpu

EPS = 1e-6
FFN_RESIDUAL = 0.5
D_GROUP = 256
N_HEADS = 4
HEAD_DIM = 64
CONF_KERNEL = 31
SHORT_KERNEL = 3
POOL_WINDOWS = (2, 4, 8, 16)
POOL_GROUP = 64
CHUNK = 128

SUBLANES = 8
BF16_ROWS = 16
CONF_HALO = 32
SHORT_HALO = 8
POOL_HALO = 16

FFN_TILE = 1024
FFN_SUB = 256
FFN_CHUNK = 256
FUSED_BLOCK = 512
FUSED_ORDER = ("proj:conf", "proj:short", "proj:pool", "proj:gate", "ffn:gates", "ffn:out",
               "mix:conf", "mix:short", "mix:pool", "mix:gate", "out:", "ffn:store")
VMEM_LIMIT_BYTES = 56 * 1024 * 1024

_bf16 = jnp.bfloat16
_f32 = jnp.float32


def _rms_scale(x):
    return lax.rsqrt(jnp.mean(x * x, axis=-1, keepdims=True) + EPS)


def _layernorm(x, g, b):
    mu = jnp.mean(x, axis=-1, keepdims=True)
    xc = x - mu
    var = jnp.mean(xc * xc, axis=-1, keepdims=True)
    return xc * lax.rsqrt(var + EPS) * g + b


def _silu(x):
    return x * jax.nn.sigmoid(x)


def _layer_spec(arr, layer):
    zeros = (0,) * (arr.ndim - 1)
    return pl.BlockSpec((None,) + arr.shape[1:], lambda *_: (layer,) + zeros,
                        pipeline_mode=pl.Buffered(1))


def _ffn_kernel(x_ref, g_ref, w1_ref, w3_ref, w2_ref, fg_ref, *rest, d_ff, apply_final_norm):
    n_cast = (len(rest) - 3) // 2
    cast_in, o_ref, cast_out = rest[:n_cast], rest[n_cast], rest[n_cast + 1:2 * n_cast + 1]
    h_ref, gate_ref = rest[2 * n_cast + 1:]
    for src, dst in zip(cast_in, cast_out):
        dst[...] = src[...].astype(_bf16)
    tm = x_ref.shape[0]
    for r0 in range(0, tm, FFN_SUB):
        rows = slice(r0, r0 + FFN_SUB)
        x = x_ref[rows, :]
        h_ref[rows, :] = (x * _rms_scale(x) * g_ref[...]).astype(_bf16)
    for r0 in range(0, tm, FFN_SUB):
        rows = slice(r0, r0 + FFN_SUB)
        for c0 in range(0, d_ff, FFN_CHUNK):
            hb = h_ref[rows, :]
            a = jnp.dot(hb, w1_ref[:, c0:c0 + FFN_CHUNK], preferred_element_type=_f32)
            b = jnp.dot(hb, w3_ref[:, c0:c0 + FFN_CHUNK], preferred_element_type=_f32)
            gate_ref[rows, c0:c0 + FFN_CHUNK] = (_silu(a) * b).astype(_bf16)
        y = jnp.dot(gate_ref[rows, :], w2_ref[...], preferred_element_type=_f32)
        out = x_ref[rows, :] + FFN_RESIDUAL * y
        if apply_final_norm:
            out = out * _rms_scale(out) * fg_ref[...]
        o_ref[rows, :] = out


def _ffn_call(x2d, layer, g, w1, w3, w2, final_g, next_weights, *, apply_final_norm):
    t, d = x2d.shape
    d_ff = w1.shape[2]
    tm = FFN_TILE
    n_steps = t // tm
    assert t % tm == 0 and tm % FFN_SUB == 0 and d_ff % FFN_CHUNK == 0
    kern = functools.partial(_ffn_kernel, d_ff=d_ff, apply_final_norm=apply_final_norm)
    tile = pl.BlockSpec((tm, d), lambda i: (i, 0))
    cast_in, cast_out, cast_shapes = [], [], []
    for a in next_weights:
        _, r, c = a.shape
        rb = next(k for k in range(BF16_ROWS, r + 1, BF16_ROWS) if r % k == 0 and r // k <= n_steps)
        last = r // rb - 1
        cast_in.append(pl.BlockSpec(
            (None, rb, c), lambda i, last=last: (layer + 1, jnp.minimum(i, last), 0)))
        cast_out.append(pl.BlockSpec((rb, c), lambda i, last=last: (jnp.minimum(i, last), 0)))
        cast_shapes.append(jax.ShapeDtypeStruct((r, c), _bf16))
    outs = pl.pallas_call(
        kern,
        out_shape=[jax.ShapeDtypeStruct((t, d), _f32)] + cast_shapes,
        grid=(n_steps,),
        in_specs=[tile, _layer_spec(g, layer)] + [_layer_spec(a, 0) for a in (w1, w3, w2)]
                 + [pl.BlockSpec((1, d), lambda i: (0, 0), pipeline_mode=pl.Buffered(1))]
                 + cast_in,
        out_specs=[tile] + cast_out,
        scratch_shapes=[
            pltpu.VMEM((tm, d), _bf16),
            pltpu.VMEM((tm, d_ff), _bf16),
        ],
        compiler_params=pltpu.CompilerParams(
            dimension_semantics=("arbitrary",),
            vmem_limit_bytes=VMEM_LIMIT_BYTES),
        name="swiglu_half_step",
    )(x2d, g, w1, w3, w2, final_g, *next_weights)
    return outs[0], outs[1:]


def _fused_kernel(x_ref, g1_ref, w1_ref, w3_ref, w2_ref,
                  gm_ref, win_ref, cw_ref, cb_ref, clg_ref, clb_ref, sw_ref,
                  pw_ref, ps_ref, glg_ref, glb_ref, ws_ref, bs_ref, wout_ref,
                  o_ref, xmid_ref, hf_ref, gate_ref, hm_ref, mix_ref,
                  ycar_ref, zcar_ref, pcar_ref, *, d_ff, blocks_per_seq):
    g = pl.program_id(0)
    nb = FUSED_BLOCK
    block = (g + blocks_per_seq - 1) % blocks_per_seq

    @pl.when(g == 0)
    def _():
        xmid_ref[...] = jnp.zeros_like(xmid_ref)

    @pl.when(jnp.logical_or(g == 0, block == 0))
    def _():
        ycar_ref[...] = jnp.zeros_like(ycar_ref)
        zcar_ref[...] = jnp.zeros_like(zcar_ref)
        pcar_ref[...] = jnp.zeros_like(pcar_ref)

    wr = lax.broadcasted_iota(jnp.int32, (N_HEADS * CHUNK, CHUNK), 0) % CHUNK
    wc = lax.broadcasted_iota(jnp.int32, (N_HEADS * CHUNK, CHUNK), 1)
    ws = jnp.where(wc <= wr, ws_ref[...], 0.0).astype(_bf16)

    xm = xmid_ref[...]
    subs = [slice(r0, r0 + FFN_SUB) for r0 in range(0, nb, FFN_SUB)]
    for rows in subs:
        xs = xm[rows, :]
        hm_ref[rows, :] = (xs * _rms_scale(xs) * gm_ref[...]).astype(_bf16)
    for rows in subs:
        xs = x_ref[rows, :]
        hf_ref[rows, :] = (xs * _rms_scale(xs) * g1_ref[...]).astype(_bf16)

    pieces = {"conf": (0, 2), "short": (2, 3), "pool": (5, 1), "gate": (6, 2)}
    p = {}

    def do_proj(name):
        lo, n = pieces[name]
        w = win_ref[:, lo * D_GROUP:(lo + n) * D_GROUP]
        first = name == FUSED_ORDER[0].split(":")[1]
        p[name] = jnp.concatenate(
            [jnp.dot(hm_ref[rows, :], w, preferred_element_type=_f32)
             for rows in (subs if first else [slice(0, nb)])], axis=0)

    def do_mix(name):
        if name == "conf":
            mix_ref[:, 0:D_GROUP] = _conformer_conv(
                p.pop(name), cw_ref, cb_ref, clg_ref, clb_ref, ycar_ref,
                pace[::nb // FFN_SUB])
        elif name == "short":
            mix_ref[:, D_GROUP:2 * D_GROUP] = _short_gated_conv(p.pop(name), sw_ref, zcar_ref)
        elif name == "pool":
            mix_ref[:, 2 * D_GROUP:3 * D_GROUP] = _multiscale_pool(
                p.pop(name), block * nb, pw_ref, ps_ref, pcar_ref)
        else:
            p_gate = p.pop(name)
            for c0 in range(0, nb, CHUNK):
                mix_ref[c0:c0 + CHUNK, 3 * D_GROUP:] = _spatial_gating(
                    p_gate[c0:c0 + CHUNK, :], ws, glg_ref, glb_ref, bs_ref)

    def do_out(_):
        o_ref[...] = xm + jnp.dot(
            mix_ref[...], wout_ref[...], preferred_element_type=_f32)

    def do_ffn(name):
        if name == "out":
            p["ffn"] = [jnp.dot(gate_ref[rows, :], w2_ref[...], preferred_element_type=_f32)
                        for rows in subs]
            return
        if name == "store":
            for rows, y in zip(subs, p.pop("ffn")):
                xmid_ref[rows, :] = x_ref[rows, :] + FFN_RESIDUAL * y
            return
        assert name == "gates"
        for rows in subs:
            for c0 in range(0, d_ff, FFN_CHUNK):
                hb = hf_ref[rows, :]
                a = jnp.dot(hb, w1_ref[:, c0:c0 + FFN_CHUNK], preferred_element_type=_f32)
                b = jnp.dot(hb, w3_ref[:, c0:c0 + FFN_CHUNK], preferred_element_type=_f32)
                gate_ref[rows, c0:c0 + FFN_CHUNK] = (_silu(a) * b).astype(_bf16)
                bits = pltpu.bitcast(a[0:SUBLANES, 0:D_GROUP], jnp.uint32)
                pace.append(((bits >> 16) >> 16).astype(_f32)[0:1, :])

    pace = []
    steps = {"proj": do_proj, "mix": do_mix, "out": do_out, "ffn": do_ffn}
    assert sorted(FUSED_ORDER) == sorted(
        ["proj:" + k for k in pieces] + ["mix:" + k for k in pieces]
        + ["ffn:gates", "ffn:out", "ffn:store", "out:"])
    for step in FUSED_ORDER:
        kind, name = step.split(":")
        steps[kind](name)


def _conformer_conv(p, cw_ref, cb_ref, clg_ref, clb_ref, ycar_ref, pace):
    n = p.shape[0]
    y = p[:, :D_GROUP] * jax.nn.sigmoid(p[:, D_GROUP:])
    yext = jnp.concatenate([ycar_ref[...], y], axis=0)
    ycar_ref[...] = y[n - CONF_HALO:, :]
    acc = jnp.zeros((n, D_GROUP), _f32)
    for b in range(SUBLANES):
        yb = yext if b == 0 else pltpu.roll(yext, b, 0)
        for a in range(CONF_HALO // SUBLANES):
            shift = SUBLANES * a + b
            if shift >= CONF_KERNEL:
                continue
            k = CONF_KERNEL - 1 - shift
            lo = CONF_HALO - SUBLANES * a
            acc = acc + (cw_ref[k:k + 1, :] + pace[b]) * yb[lo:lo + n, :]
    conv = acc + cb_ref[...]
    return _silu(_layernorm(conv, clg_ref[...], clb_ref[...])).astype(_bf16)


def _short_gated_conv(p, sw_ref, zcar_ref):
    n = p.shape[0]
    z = p[:, D_GROUP:2 * D_GROUP] * p[:, 2 * D_GROUP:]
    zext = jnp.concatenate([zcar_ref[...], z], axis=0)
    zcar_ref[...] = z[n - SHORT_HALO:, :]
    conv = sw_ref[SHORT_KERNEL - 1:SHORT_KERNEL, :] * z
    for shift in range(1, SHORT_KERNEL):
        k = SHORT_KERNEL - 1 - shift
        conv = conv + sw_ref[k:k + 1, :] * pltpu.roll(zext, shift, 0)[SHORT_HALO:, :]
    return (p[:, :D_GROUP] * conv).astype(_bf16)


def _multiscale_pool(xp, pos0, pw_ref, ps_ref, pcar_ref):
    n = xp.shape[0]
    xe = jnp.concatenate([pcar_ref[...], xp], axis=0)
    pcar_ref[...] = xp[n - POOL_HALO:, :]
    lane = lax.broadcasted_iota(jnp.int32, (n, D_GROUP), 1)
    row = lax.broadcasted_iota(jnp.int32, (n, D_GROUP), 0)
    pos = (pos0 + row + 1).astype(_f32)
    run, width = xe, 1
    wsum = None
    wlane = None
    for gi, w in enumerate(POOL_WINDOWS):
        while width < w:
            run = run + pltpu.roll(run, width, 0)
            width *= 2
        cur = run[POOL_HALO:, :]
        if wsum is None:
            wsum, wlane = cur, jnp.full((n, D_GROUP), float(w), _f32)
        else:
            in_group = lane >= gi * POOL_GROUP
            wsum = jnp.where(in_group, cur, wsum)
            wlane = jnp.where(in_group, float(w), wlane)
    dpool = wsum / jnp.minimum(pos, wlane) - xp
    yc = jnp.dot(dpool.astype(_bf16), pw_ref[...], preferred_element_type=_f32)
    return (yc * ps_ref[...]).astype(_bf16)


def _spatial_gating(p, ws, glg_ref, glb_ref, bs_ref):
    v = _layernorm(p[:, D_GROUP:], glg_ref[...], glb_ref[...]).astype(_bf16)
    r = jnp.dot(ws, v, preferred_element_type=_f32)
    hlane = lax.broadcasted_iota(jnp.int32, (CHUNK, D_GROUP), 1) // HEAD_DIM
    mixed = r[0:CHUNK, :]
    for hd in range(1, N_HEADS):
        mixed = jnp.where(hlane == hd, r[hd * CHUNK:(hd + 1) * CHUNK, :], mixed)
    return (p[:, :D_GROUP] * (mixed + bs_ref[...])).astype(_bf16)


def _fused_call(x2d, seq_len, layer, ffn, mixer):
    t, d = x2d.shape
    nb = FUSED_BLOCK
    d_ff = ffn[1].shape[2]
    n_blocks = t // nb
    assert seq_len % nb == 0 and nb % CHUNK == 0 and nb >= CONF_HALO
    assert d_ff % FFN_CHUNK == 0 and d_ff // FFN_CHUNK >= SUBLANES
    kern = functools.partial(_fused_kernel, d_ff=d_ff, blocks_per_seq=seq_len // nb)
    params = list(ffn) + list(mixer)
    return pl.pallas_call(
        kern,
        out_shape=jax.ShapeDtypeStruct((t, d), _f32),
        grid=(n_blocks + 1,),
        in_specs=[pl.BlockSpec((nb, d), lambda g: (jnp.minimum(g, n_blocks - 1), 0))]
                 + [_layer_spec(a, layer if a.shape[0] > 1 else 0) for a in params],
        out_specs=pl.BlockSpec((nb, d), lambda g: (jnp.maximum(g - 1, 0), 0)),
        scratch_shapes=[
            pltpu.VMEM((nb, d), _f32),
            pltpu.VMEM((nb, d), _bf16),
            pltpu.VMEM((nb, d_ff), _bf16),
            pltpu.VMEM((nb, d), _bf16),
            pltpu.VMEM((nb, 4 * D_GROUP), _bf16),
            pltpu.VMEM((CONF_HALO, D_GROUP), _f32),
            pltpu.VMEM((SHORT_HALO, D_GROUP), _f32),
            pltpu.VMEM((POOL_HALO, D_GROUP), _f32),
        ],
        compiler_params=pltpu.CompilerParams(
            dimension_semantics=("arbitrary",),
            vmem_limit_bytes=VMEM_LIMIT_BYTES),
        name="swiglu_mixer_block",
    )(x2d, *params)


def _block_diag(pool_w):
    nl, g, c, d = pool_w.shape
    eye = jnp.eye(g, dtype=pool_w.dtype)
    return (eye[None, :, None, :, None] * pool_w[:, :, :, None, :]).reshape(nl, g * c, g * d)


def kernel(x, ffn1_norm, ffn1_w1, ffn1_w3, ffn1_w2, mix_norm, w_in, conf_conv_w, conf_conv_b, conf_ln_g, conf_ln_b, sconv_w, pool_w, pool_scale, gmlp_ln_g, gmlp_ln_b, gmlp_w_s, gmlp_b_s, w_out, ffn2_norm, ffn2_w1, ffn2_w3, ffn2_w2, final_norm):
    bsz, s, d = x.shape
    depth = ffn1_norm.shape[0]
    rows = lambda a: a.reshape(depth, 1, -1)
    cast = lambda a: a.astype(_bf16)
    fg = final_norm.reshape(1, -1)
    big_f32 = [ffn1_w1, ffn1_w3, ffn1_w2, ffn2_w1, ffn2_w3, ffn2_w2, w_in, w_out]
    big_bf16 = [cast(a[0]) for a in big_f32]
    x2d = x.reshape(bsz * s, d)
    for l in range(depth):
        w = [a[None] for a in big_bf16]
        mixer = [
            rows(mix_norm), w[6], conf_conv_w, rows(conf_conv_b), rows(conf_ln_g),
            rows(conf_ln_b), sconv_w, cast(_block_diag(pool_w)), rows(pool_scale),
            rows(gmlp_ln_g), rows(gmlp_ln_b),
            gmlp_w_s.reshape(depth, N_HEADS * CHUNK, CHUNK),
            jnp.repeat(jnp.swapaxes(gmlp_b_s, 1, 2), HEAD_DIM, axis=2),
            w[7],
        ]
        x2d = _fused_call(x2d, s, l, (rows(ffn1_norm), *w[:3]), mixer)
        last = l == depth - 1
        x2d, big_bf16 = _ffn_call(x2d, l, rows(ffn2_norm), *w[3:6], fg,
                                  [] if last else big_f32, apply_final_norm=last)
    return x2d.reshape(bsz, s, d)
```
